```python
import math
import jax, jax.numpy as jnp
from jax import lax
import numpy as np

D_MODEL = 1024
BATCH = 4
SEQ = 8192
DEPTH = 4

N_MIXERS = 4
N_HEADS = 8
HEAD_DIM = D_MODEL // N_HEADS
ROPE_THETA = 10000.0
Q_BLOCK = 128
HG_CHUNK = 64
DA_HALF = HEAD_DIM // 2
NSA_GROUPS = 2
NSA_HPG = N_HEADS // NSA_GROUPS
CMP_BLOCK = 32
CMP_STRIDE = 16
SLC_BLOCK = 64
SLC_TOPK = 16
WINDOW = 512
NSA_Q_BLOCK = 64
D_FF = 2816
N_EXPERTS = 8
TOP_K = 2
D_FF_EXPERT = 3584
MOE_BLOCK = 128
LN_EPS = 1e-5
NEG = -1e30
FORCE = 1e9
DEEPNORM_ALPHA = (2 * DEPTH) ** 0.25
DEEPNORM_BETA = (8 * DEPTH) ** -0.25

kernel_name = 'hybrid_hgrn2_diff_nsa_stickbreak_moe'


def layer_norm(x, g, b):
    xf = x.astype(jnp.float32)
    mu = jnp.mean(xf, axis=-1, keepdims=True)
    xc = xf - mu
    var = jnp.mean(xc * xc, axis=-1, keepdims=True)
    return (xc * lax.rsqrt(var + LN_EPS) * g.astype(jnp.float32) + b.astype(jnp.float32)).astype(x.dtype)


def rms_norm(x, g):
    xf = x.astype(jnp.float32)
    return xf * lax.rsqrt(jnp.mean(xf * xf, axis=-1, keepdims=True) + LN_EPS) * g.astype(jnp.float32)


def rope(x, pos):
    d = x.shape[-1]
    half = d // 2
    inv = ROPE_THETA ** (-jnp.arange(half, dtype=jnp.float32) / half)
    ang = pos.astype(jnp.float32)[:, None] * inv[None, :]
    shp = (1, x.shape[1]) + (1,) * (x.ndim - 3) + (half,)
    cos = jnp.cos(ang).reshape(shp)
    sin = jnp.sin(ang).reshape(shp)
    xf = x.astype(jnp.float32)
    x1, x2 = xf[..., :half], xf[..., half:]
    return jnp.concatenate([x1 * cos - x2 * sin, x2 * cos + x1 * sin], axis=-1).astype(x.dtype)


def masked_softmax(s, mask):
    s = jnp.where(mask, s.astype(jnp.float32), NEG)
    p = jax.nn.softmax(s, axis=-1)
    return jnp.where(mask, p, 0.0)


def hgrn2_mixer(h, w_in, lb_logits, norm_g, w_out, layer):
    b, s, _ = h.shape
    nh, dk = N_HEADS, HEAD_DIM
    c = HG_CHUNK
    nc = s // c
    q, f, i, g = jnp.split(h @ w_in, 4, axis=-1)
    lb = jnp.cumsum(jax.nn.softmax(lb_logits.astype(jnp.float32), axis=0), axis=0)[layer]
    forget = lb + (1.0 - lb) * jax.nn.sigmoid(f.astype(jnp.float32))
    log_f = jnp.log(forget)
    k = 1.0 - forget
    q = jax.nn.silu(q.astype(jnp.float32))
    v = i.astype(jnp.float32)

    def to_chunks(t):
        return t.reshape(b, nc, c, nh, dk).transpose(1, 0, 3, 2, 4)

    tri = jnp.tril(jnp.ones((c, c), bool))[:, :, None]

    def step(state, inp):
        qc, kc, vc, lf = inp
        cum = jnp.cumsum(lf, axis=2)
        rel = jnp.where(tri, cum[:, :, :, None, :] - cum[:, :, None, :, :], -jnp.inf)
        scores = jnp.einsum('bhtd,bhsd,bhtsd->bhts', qc, kc, jnp.exp(rel))
        out = scores @ vc + jnp.einsum('bhtd,bhdv->bhtv', qc * jnp.exp(cum), state)
        last = cum[:, :, -1:, :]
        new_state = jnp.exp(last[:, :, 0, :, None]) * state + jnp.einsum('bhsd,bhsv->bhdv', kc * jnp.exp(last - cum), vc)
        return new_state, out

    state0 = jnp.zeros((b, nh, dk, dk), jnp.float32)
    _, o = lax.scan(step, state0, (to_chunks(q), to_chunks(k), to_chunks(v), to_chunks(log_f)))
    o = o.transpose(1, 0, 3, 2, 4).reshape(b, s, nh, dk)
    o = rms_norm(o, norm_g) * jax.nn.sigmoid(g.astype(jnp.float32)).reshape(b, s, nh, dk)
    return o.reshape(b, s, D_MODEL).astype(h.dtype) @ w_out


def diff_attention_mixer(h, w_in, lam, norm_g, w_out, pos, layer):
    b, s, _ = h.shape
    q, k, v = jnp.split(h @ w_in, 3, axis=-1)
    q = rope(q.reshape(b, s, N_HEADS, 2, DA_HALF), pos).transpose(0, 2, 3, 1, 4)
    k = rope(k.reshape(b, s, N_HEADS, 2, DA_HALF), pos).transpose(0, 2, 3, 1, 4)
    v = v.reshape(b, s, N_HEADS, HEAD_DIM).transpose(0, 2, 1, 3)
    lam_init = 0.8 - 0.6 * math.exp(-0.3 * layer)
    lf = lam.astype(jnp.float32)
    lmb = jnp.exp(jnp.sum(lf[0] * lf[1])) - jnp.exp(jnp.sum(lf[2] * lf[3])) + lam_init
    kpos = jnp.arange(s)
    scale = DA_HALF ** -0.5

    def block(bi):
        s0 = bi * Q_BLOCK
        qb = lax.dynamic_slice_in_dim(q, s0, Q_BLOCK, axis=3)
        sc = jnp.einsum('bhcqd,bhckd->bhcqk', qb, k).astype(jnp.float32) * scale
        mask = (s0 + jnp.arange(Q_BLOCK))[:, None] >= kpos[None, :]
        p = masked_softmax(sc, mask)
        attn = p[:, :, 0] - lmb * p[:, :, 1]
        return jnp.einsum('bhqk,bhkd->bqhd', attn.astype(v.dtype), v)

    o = lax.map(block, jnp.arange(s // Q_BLOCK))
    o = o.transpose(1, 0, 2, 3, 4).reshape(b, s, N_HEADS, HEAD_DIM)
    o = rms_norm(o, norm_g) * (1.0 - lam_init)
    return o.reshape(b, s, D_MODEL).astype(h.dtype) @ w_out


def nsa_mixer(h, w_in, cmp_pe, cmp_w1, cmp_w2, w_out, pos):
    b, s, _ = h.shape
    G, HG, dk = NSA_GROUPS, NSA_HPG, HEAD_DIM
    proj = h @ w_in
    q, kv, gates = jnp.split(proj, [D_MODEL, D_MODEL + 6 * G * dk], axis=-1)
    q = q.reshape(b, s, N_HEADS, dk)
    kv = kv.reshape(b, s, 6, G, dk)
    k_cmp, v_cmp, k_slc, v_slc, k_win, v_win = [kv[:, :, j] for j in range(6)]
    gates = jax.nn.sigmoid(gates.astype(jnp.float32)).reshape(b, s, N_HEADS, 3)
    q_rot = rope(q, pos)
    k_slc = rope(k_slc, pos)
    k_win = rope(k_win, pos)

    n_cmp = (s - CMP_BLOCK) // CMP_STRIDE + 1
    tok_idx = np.arange(n_cmp)[:, None] * CMP_STRIDE + np.arange(CMP_BLOCK)[None, :]

    def compress(t, j):
        blk = t[:, tok_idx] + cmp_pe[j][None, None, :, None, :]
        blk = blk.transpose(0, 3, 1, 2, 4).reshape(b, G, n_cmp, CMP_BLOCK * dk)
        return jax.nn.gelu(blk @ cmp_w1[j]) @ cmp_w2[j]

    kc = compress(k_cmp, 0)
    vc = compress(v_cmp, 1)
    cmp_end = jnp.asarray(np.arange(n_cmp) * CMP_STRIDE + CMP_BLOCK - 1)
    n_slc = s // SLC_BLOCK
    cs = np.arange(n_cmp) * CMP_STRIDE
    ss_ = np.arange(n_slc) * SLC_BLOCK
    ov = np.clip(np.minimum(cs[:, None] + CMP_BLOCK, ss_[None, :] + SLC_BLOCK) - np.maximum(cs[:, None], ss_[None, :]), 0, None) / CMP_BLOCK
    cmp_to_slc = jnp.asarray(ov, jnp.float32)
    n_sel = min(SLC_TOPK, n_slc)
    ks_blk = k_slc.transpose(0, 2, 1, 3).reshape(b, G, n_slc, SLC_BLOCK * dk)
    vs_blk = v_slc.transpose(0, 2, 1, 3).reshape(b, G, n_slc, SLC_BLOCK * dk)
    kw = jnp.pad(k_win.transpose(0, 2, 1, 3), ((0, 0), (0, 0), (WINDOW, 0), (0, 0)))
    vw = jnp.pad(v_win.transpose(0, 2, 1, 3), ((0, 0), (0, 0), (WINDOW, 0), (0, 0)))
    qg = q.reshape(b, s, G, HG, dk).transpose(0, 2, 3, 1, 4)
    qrg = q_rot.reshape(b, s, G, HG, dk).transpose(0, 2, 3, 1, 4)
    scale = dk ** -0.5
    blk_id = jnp.arange(n_slc)
    qb_n = NSA_Q_BLOCK

    def block(bi):
        s0 = bi * qb_n
        tq = s0 + jnp.arange(qb_n)
        qb = lax.dynamic_slice_in_dim(qg, s0, qb_n, axis=3)
        qrb = lax.dynamic_slice_in_dim(qrg, s0, qb_n, axis=3)
        sc = jnp.einsum('bgiqd,bgcd->bgiqc', qb, kc).astype(jnp.float32) * scale
        pc = masked_softmax(sc, cmp_end[None, :] <= tq[:, None])
        o_cmp = jnp.einsum('bgiqc,bgcd->bgiqd', pc.astype(vc.dtype), vc)
        imp = jnp.einsum('bgqc,cn->bgqn', pc.sum(axis=2), cmp_to_slc)
        cur = tq // SLC_BLOCK
        forced = (blk_id[None, :] == 0) | (blk_id[None, :] == cur[:, None]) | (blk_id[None, :] == cur[:, None] - 1)
        imp = jnp.where(forced, FORCE, imp)
        imp = jnp.where(blk_id[None, :] <= cur[:, None], imp, NEG)
        top_s, top_i = lax.top_k(imp, n_sel)
        sel_ok = top_s > 0.5 * NEG
        gidx = top_i.reshape(b, G, qb_n * n_sel, 1)
        kg = jnp.take_along_axis(ks_blk, gidx, axis=2).reshape(b, G, qb_n, n_sel * SLC_BLOCK, dk)
        vg = jnp.take_along_axis(vs_blk, gidx, axis=2).reshape(b, G, qb_n, n_sel * SLC_BLOCK, dk)
        tok = top_i[..., None] * SLC_BLOCK + jnp.arange(SLC_BLOCK)
        ms = (sel_ok[..., None] & (tok <= tq[None, None, :, None, None])).reshape(b, G, 1, qb_n, n_sel * SLC_BLOCK)
        sg = jnp.einsum('bgiqd,bgqmd->bgiqm', qrb, kg).astype(jnp.float32) * scale
        ps = masked_softmax(sg, ms)
        o_slc = jnp.einsum('bgiqm,bgqmd->bgiqd', ps.astype(vg.dtype), vg)
        kwb = lax.dynamic_slice_in_dim(kw, s0, qb_n + WINDOW, axis=2)
        vwb = lax.dynamic_slice_in_dim(vw, s0, qb_n + WINDOW, axis=2)
        kp = s0 - WINDOW + jnp.arange(qb_n + WINDOW)
        mw = (kp[None, :] <= tq[:, None]) & (kp[None, :] > tq[:, None] - WINDOW) & (kp[None, :] >= 0)
        sw = jnp.einsum('bgiqd,bgkd->bgiqk', qrb, kwb).astype(jnp.float32) * scale
        pw = masked_softmax(sw, mw)
        o_win = jnp.einsum('bgiqk,bgkd->bgiqd', pw.astype(vwb.dtype), vwb)
        o = jnp.stack([o_cmp, o_slc, o_win], axis=-2)
        return o.transpose(0, 3, 1, 2, 4, 5).reshape(b, qb_n, N_HEADS, 3, dk)

    o = lax.map(block, jnp.arange(s // qb_n))
    o = o.transpose(1, 0, 2, 3, 4, 5).reshape(b, s, N_HEADS, 3, dk)
    o = jnp.einsum('bshc,bshcd->bshd', gates.astype(o.dtype), o)
    return o.reshape(b, s, D_MODEL) @ w_out


def stick_breaking_mixer(h, w_in, w_out):
    b, s, _ = h.shape
    q, k, v = jnp.split(h @ w_in, 3, axis=-1)
    q = q.reshape(b, s, N_HEADS, HEAD_DIM).transpose(0, 2, 1, 3)
    k = k.reshape(b, s, N_HEADS, HEAD_DIM).transpose(0, 2, 1, 3)
    v = v.reshape(b, s, N_HEADS, HEAD_DIM).transpose(0, 2, 1, 3)
    kpos = jnp.arange(s)
    scale = HEAD_DIM ** -0.5

    def block(bi):
        s0 = bi * Q_BLOCK
        qb = lax.dynamic_slice_in_dim(q, s0, Q_BLOCK, axis=2)
        z = jnp.einsum('bhqd,bhkd->bhqk', qb, k).astype(jnp.float32) * scale
        tq = s0 + jnp.arange(Q_BLOCK)
        strict = kpos[None, :] < tq[:, None]
        log_keep = jnp.where(strict, jax.nn.log_sigmoid(-z), 0.0)
        after = lax.cumsum(log_keep, axis=3, reverse=True) - log_keep
        a = jnp.where(strict, jnp.exp(jax.nn.log_sigmoid(z) + after), 0.0)
        return jnp.einsum('bhqk,bhkd->bqhd', a.astype(v.dtype), v)

    o = lax.map(block, jnp.arange(s // Q_BLOCK))
    o = o.transpose(1, 0, 2, 3, 4).reshape(b, s, D_MODEL)
    return o @ w_out


def swiglu(h, w_gate, w_up, w_down):
    return (jax.nn.silu(h @ w_gate) * (h @ w_up)) @ w_down


def moe_swiglu(h, w_router, w_gate, w_up, w_down):
    b, s, d = h.shape
    t = b * s
    hf = h.reshape(t, d)
    logits = (hf @ w_router).astype(jnp.float32)
    top_val, top_idx = lax.top_k(logits, TOP_K)
    gates = jax.nn.softmax(top_val, axis=-1)
    n_assign = t * TOP_K
    flat_e = top_idx.reshape(n_assign)
    flat_tok = jnp.repeat(jnp.arange(t, dtype=jnp.int32), TOP_K)
    order = jnp.argsort(flat_e)
    sorted_e = flat_e[order]
    sorted_tok = flat_tok[order]
    counts = jnp.bincount(flat_e, length=N_EXPERTS)
    padded = (counts + MOE_BLOCK - 1) // MOE_BLOCK * MOE_BLOCK
    pad_end = jnp.cumsum(padded)
    pad_start = pad_end - padded
    grp_start = jnp.cumsum(counts) - counts
    dest = (pad_start[sorted_e] + jnp.arange(n_assign) - grp_start[sorted_e]).astype(jnp.int32)
    n_blocks = -(-n_assign // MOE_BLOCK) + N_EXPERTS
    n_slots = n_blocks * MOE_BLOCK
    slot_tok = jnp.zeros((n_slots,), jnp.int32).at[dest].set(sorted_tok)
    block_e = jnp.minimum(jnp.searchsorted(pad_end, jnp.arange(n_blocks) * MOE_BLOCK, side='right'), N_EXPERTS - 1)
    xs = hf[slot_tok].reshape(n_blocks, MOE_BLOCK, d)

    def expert_block(args):
        xb, e = args
        return (jax.nn.silu(xb @ w_gate[e]) * (xb @ w_up[e])) @ w_down[e]

    ys = lax.map(expert_block, (xs, block_e)).reshape(n_slots, d)
    slot_of_assign = jnp.zeros((n_assign,), jnp.int32).at[order].set(dest)
    y = ys[slot_of_assign].reshape(t, TOP_K, d)
    out = jnp.einsum('tk,tkd->td', gates.astype(y.dtype), y)
    return out.reshape(b, s, d)


def setup_inputs(seed: int = 0) -> dict:
    key = jax.random.key(seed)
    ks = jax.random.split(key, 25)
    D = D_MODEL
    f32 = jnp.float32

    def nrm(k, shape, scale):
        return jax.random.normal(k, shape, f32) * scale

    nsa_in = D + 6 * NSA_GROUPS * HEAD_DIM + 3 * N_HEADS
    out_scale = DEEPNORM_BETA * D ** -0.5
    return {
        'x': nrm(ks[0], (BATCH, SEQ, D), 1.0),
        'hg_w_in': nrm(ks[1], (D, 4 * D), D ** -0.5),
        'hg_lb': nrm(ks[2], (DEPTH + 1, D), 0.1),
        'hg_norm_g': 1.0 + nrm(ks[3], (HEAD_DIM,), 0.01),
        'hg_w_out': nrm(ks[4], (D, D), out_scale),
        'da_w_in': nrm(ks[5], (D, 3 * D), D ** -0.5),
        'da_lam': nrm(ks[6], (4, DA_HALF), 0.1),
        'da_norm_g': 1.0 + nrm(ks[7], (HEAD_DIM,), 0.01),
        'da_w_out': nrm(ks[8], (D, D), out_scale),
        'nsa_w_in': nrm(ks[9], (D, nsa_in), D ** -0.5),
        'nsa_cmp_pe': nrm(ks[10], (2, CMP_BLOCK, HEAD_DIM), 0.02),
        'nsa_cmp_w1': nrm(ks[11], (2, CMP_BLOCK * HEAD_DIM, HEAD_DIM), (CMP_BLOCK * HEAD_DIM) ** -0.5),
        'nsa_cmp_w2': nrm(ks[12], (2, HEAD_DIM, HEAD_DIM), HEAD_DIM ** -0.5),
        'nsa_w_out': nrm(ks[13], (D, D), out_scale),
        'sb_w_in': nrm(ks[14], (D, 3 * D), D ** -0.5),
        'sb_w_out': nrm(ks[15], (D, D), out_scale),
        'ffn_w_gate': nrm(ks[16], (DEPTH // 2, D, D_FF), D ** -0.5),
        'ffn_w_up': nrm(ks[17], (DEPTH // 2, D, D_FF), D ** -0.5),
        'ffn_w_down': nrm(ks[18], (DEPTH // 2, D_FF, D), DEEPNORM_BETA * D_FF ** -0.5),
        'moe_w_router': nrm(ks[19], (DEPTH // 2, D, N_EXPERTS), D ** -0.5),
        'moe_w_gate': nrm(ks[20], (DEPTH // 2, N_EXPERTS, D, D_FF_EXPERT), D ** -0.5),
        'moe_w_up': nrm(ks[21], (DEPTH // 2, N_EXPERTS, D, D_FF_EXPERT), D ** -0.5),
        'moe_w_down': nrm(ks[22], (DEPTH // 2, N_EXPERTS, D_FF_EXPERT, D), DEEPNORM_BETA * D_FF_EXPERT ** -0.5),
        'ln_g': 1.0 + nrm(ks[23], (DEPTH, 2, D), 0.01),
        'ln_b': nrm(ks[24], (DEPTH, 2, D), 0.01),
    }


def reference(x, hg_w_in, hg_lb, hg_norm_g, hg_w_out, da_w_in, da_lam, da_norm_g, da_w_out,
              nsa_w_in, nsa_cmp_pe, nsa_cmp_w1, nsa_cmp_w2, nsa_w_out, sb_w_in, sb_w_out,
              ffn_w_gate, ffn_w_up, ffn_w_down, moe_w_router, moe_w_gate, moe_w_up, moe_w_down,
              ln_g, ln_b):
    pos = jnp.arange(x.shape[1])
    h = x
    for layer in range(DEPTH):
        m = layer % N_MIXERS
        if m == 0:
            y = hgrn2_mixer(h, hg_w_in, hg_lb, hg_norm_g, hg_w_out, layer)
        elif m == 1:
            y = diff_attention_mixer(h, da_w_in, da_lam, da_norm_g, da_w_out, pos, layer)
        elif m == 2:
            y = nsa_mixer(h, nsa_w_in, nsa_cmp_pe, nsa_cmp_w1, nsa_cmp_w2, nsa_w_out, pos)
        else:
            y = stick_breaking_mixer(h, sb_w_in, sb_w_out)
        h = layer_norm(DEEPNORM_ALPHA * h + y.astype(h.dtype), ln_g[layer, 0], ln_b[layer, 0])
        j = layer // 2
        if layer % 2 == 0:
            y = swiglu(h, ffn_w_gate[j], ffn_w_up[j], ffn_w_down[j])
        else:
            y = moe_swiglu(h, moe_w_router[j], moe_w_gate[j], moe_w_up[j], moe_w_down[j])
        h = layer_norm(DEEPNORM_ALPHA * h + y.astype(h.dtype), ln_g[layer, 1], ln_b[layer, 1])
    return h
```

```python
import functools
import math

import jax
import jax.numpy as jnp
from jax import lax
from jax.experimental import pallas as pl
from jax.experimental.pallas import tpu as pltpu

F32 = jnp.float32
BF16 = jnp.bfloat16

LANE = 128
VMEM_LIMIT_BYTES = 56 * 1024 * 1024

N_HEADS = 8
HEAD_DIM = 128
ROPE_THETA = 10000.0
HG_CHUNK = 64
HG_SUB = 16
DA_HALF = 64
NSA_GROUPS = 2
NSA_HPG = 4
CMP_BLOCK = 32
CMP_STRIDE = 16
SLC_BLOCK = 64
SLC_TOPK = 16
WINDOW = 512
N_EXPERTS = 8
TOP_K = 2
LN_EPS = 1e-5
NEG = -1e30
FORCE = 1e9
DEPTH = 4
DEEPNORM_ALPHA = (2 * DEPTH) ** 0.25

_NT = (((1,), (1,)), ((), ()))


def _params(*semantics):
    return pltpu.CompilerParams(dimension_semantics=semantics, vmem_limit_bytes=VMEM_LIMIT_BYTES)


def _sigmoid(x):
    return 1.0 / (1.0 + jnp.exp(-x))


def _layer_norm(z, g, b):
    mu = jnp.mean(z, axis=-1, keepdims=True)
    zc = z - mu
    var = jnp.mean(zc * zc, axis=-1, keepdims=True)
    return zc * lax.rsqrt(var + LN_EPS) * g + b


def _proj_kernel(*refs, n_aux, groups, epilogue):
    x_ref, w_ref = refs[0], refs[1]
    aux_refs = refs[2:2 + n_aux]
    out_refs = refs[2 + n_aux:]
    x = x_ref[...].astype(BF16)
    aux = [r[...] for r in aux_refs]
    c = 0
    for out_ref, n in zip(out_refs, groups):
        for local in range(n):
            val = jnp.dot(x, w_ref[:, c * LANE:(c + 1) * LANE], preferred_element_type=F32)
            if epilogue is not None:
                val = epilogue(c, val, aux)
            out_ref[local] = val.astype(out_ref.dtype)
            c += 1


def _project_heads(x, w, groups, dtypes, *, seq, aux=(), epilogue=None, tm=512):
    t, d = x.shape
    n_chunks = sum(groups)
    assert w.shape == (d, n_chunks * LANE) and t % tm == 0 and seq % tm == 0
    tiles_per_seq = seq // tm
    in_specs = [pl.BlockSpec((tm, d), lambda i: (i, 0)),
                pl.BlockSpec((d, n_chunks * LANE), lambda i: (0, 0))]
    for a in aux:
        assert a.shape == (seq, LANE)
        in_specs.append(pl.BlockSpec((tm, LANE), lambda i: (i % tiles_per_seq, 0)))
    out_shape = [jax.ShapeDtypeStruct((n, t, LANE), dt) for n, dt in zip(groups, dtypes)]
    out_specs = [pl.BlockSpec((n, tm, LANE), lambda i: (0, i, 0)) for n in groups]
    return pl.pallas_call(
        functools.partial(_proj_kernel, n_aux=len(aux), groups=tuple(groups), epilogue=epilogue),
        grid=(t // tm,),
        in_specs=in_specs,
        out_specs=out_specs,
        out_shape=out_shape,
        compiler_params=_params("parallel"),
        name="project_heads",
    )(x, w, *aux)


def _outproj_ln_kernel(*refs, with_router):
    if with_router:
        o_ref, w_ref, h_ref, g_ref, b_ref, wr_ref, out_ref, outb_ref, logit_ref = refs
    else:
        o_ref, w_ref, h_ref, g_ref, b_ref, out_ref = refs
    y = jnp.dot(o_ref[...], w_ref[...], preferred_element_type=F32)
    hn = _layer_norm(DEEPNORM_ALPHA * h_ref[...] + y, g_ref[...], b_ref[...])
    out_ref[...] = hn
    if with_router:
        hb = hn.astype(BF16)
        outb_ref[...] = hb
        logit_ref[...] = jnp.dot(hb, wr_ref[...], preferred_element_type=F32)


def _outproj_ln(o, w, h, g, b, w_router=None, *, tm=512):
    t, d = h.shape
    assert o.shape == (t, d) and t % tm == 0
    with_router = w_router is not None
    row = lambda i: (i, 0)
    fixed = lambda i: (0, 0)
    in_specs = [pl.BlockSpec((tm, d), row), pl.BlockSpec((d, d), fixed), pl.BlockSpec((tm, d), row),
                pl.BlockSpec((1, d), fixed), pl.BlockSpec((1, d), fixed)]
    args = [o, w, h, g.reshape(1, d), b.reshape(1, d)]
    out_shape = [jax.ShapeDtypeStruct((t, d), F32)]
    out_specs = [pl.BlockSpec((tm, d), row)]
    if with_router:
        in_specs.append(pl.BlockSpec((d, LANE), fixed))
        args.append(w_router)
        out_shape += [jax.ShapeDtypeStruct((t, d), BF16), jax.ShapeDtypeStruct((t, LANE), F32)]
        out_specs += [pl.BlockSpec((tm, d), row), pl.BlockSpec((tm, LANE), row)]
    res = pl.pallas_call(
        functools.partial(_outproj_ln_kernel, with_router=with_router),
        grid=(t // tm,),
        in_specs=in_specs,
        out_specs=out_specs,
        out_shape=out_shape,
        compiler_params=_params("parallel"),
        name="outproj_ln",
    )(*args)
    return res if with_router else res[0]


def _ffn_kernel(h_ref, wg_ref, wu_ref, wd_ref, g_ref, b_ref, out_ref, xb_ref, acc_ref):
    k = pl.program_id(1)

    @pl.when(k == 0)
    def _():
        xb_ref[...] = h_ref[...].astype(BF16)
        acc_ref[...] = jnp.zeros_like(acc_ref)

    x = xb_ref[...]
    gate = jnp.dot(x, wg_ref[...], preferred_element_type=F32)
    up = jnp.dot(x, wu_ref[...], preferred_element_type=F32)
    act = (gate * _sigmoid(gate) * up).astype(BF16)
    acc_ref[...] += jnp.dot(act, wd_ref[...], preferred_element_type=F32)

    @pl.when(k == pl.num_programs(1) - 1)
    def _():
        out_ref[...] = _layer_norm(DEEPNORM_ALPHA * h_ref[...] + acc_ref[...], g_ref[...], b_ref[...])


def _ffn_ln(h, wg, wu, wd, g, b, *, tm=512, tf=256):
    t, d = h.shape
    ff = wg.shape[1]
    assert t % tm == 0 and ff % tf == 0
    return pl.pallas_call(
        _ffn_kernel,
        grid=(t // tm, ff // tf),
        in_specs=[pl.BlockSpec((tm, d), lambda i, k: (i, 0)),
                  pl.BlockSpec((d, tf), lambda i, k: (0, k)),
                  pl.BlockSpec((d, tf), lambda i, k: (0, k)),
                  pl.BlockSpec((tf, d), lambda i, k: (k, 0)),
                  pl.BlockSpec((1, d), lambda i, k: (0, 0)),
                  pl.BlockSpec((1, d), lambda i, k: (0, 0))],
        out_specs=pl.BlockSpec((tm, d), lambda i, k: (i, 0)),
        out_shape=jax.ShapeDtypeStruct((t, d), F32),
        scratch_shapes=[pltpu.VMEM((tm, d), BF16), pltpu.VMEM((tm, d), F32)],
        compiler_params=_params("parallel", "arbitrary"),
        name="swiglu_ln",
    )(h, wg, wu, wd, g.reshape(1, d), b.reshape(1, d))


def _hgrn_kernel(lb_ref, ng_ref, q_ref, f_ref, i_ref, g_ref, o_ref, state_ref, *, rows):
    c, sub = HG_CHUNK, HG_SUB

    @pl.when(pl.program_id(2) == 0)
    def _():
        state_ref[...] = jnp.zeros_like(state_ref)

    lb = lb_ref[0]
    ng = ng_ref[...]
    r_io = lax.broadcasted_iota(jnp.int32, (c, c), 0)
    c_io = lax.broadcasted_iota(jnp.int32, (c, c), 1)
    tri = (r_io >= c_io).astype(BF16)
    row_c = lax.broadcasted_iota(jnp.int32, (c, LANE), 0)
    row_s = lax.broadcasted_iota(jnp.int32, (sub, LANE), 0)

    def chunk(ci, carry):
        r0 = pl.multiple_of(ci * c, c)
        qr = q_ref[0, pl.ds(r0, c), :]
        fr = f_ref[0, pl.ds(r0, c), :]
        v = i_ref[0, pl.ds(r0, c), :]
        gr = g_ref[0, pl.ds(r0, c), :]
        forget = lb + (1.0 - lb) * _sigmoid(fr)
        lf = jnp.log(forget)
        k = 1.0 - forget
        q = qr * _sigmoid(qr)
        lf_hi = lf.astype(BF16)
        lf_lo = (lf - lf_hi.astype(F32)).astype(BF16)
        cum = (jnp.dot(tri, lf_hi, preferred_element_type=F32)
               + jnp.dot(tri, lf_lo, preferred_element_type=F32))
        v_b = v.astype(BF16)
        state_t = state_ref[...]

        qe = (q * jnp.exp(cum)).astype(BF16)
        out = lax.dot_general(qe, state_t.astype(BF16), _NT, preferred_element_type=F32)

        a_rows = [jnp.zeros((sub, c), F32)]
        for i in range(1, c // sub):
            anchor = cum[i * sub - 1:i * sub, :]
            kd = jnp.where(row_c < i * sub, k * jnp.exp(jnp.minimum(anchor - cum, 0.0)), 0.0)
            qd = q[i * sub:(i + 1) * sub] * jnp.exp(cum[i * sub:(i + 1) * sub] - anchor)
            a_rows.append(lax.dot_general(qd.astype(BF16), kd.astype(BF16), _NT,
                                          preferred_element_type=F32))
        a_off = jnp.concatenate(a_rows, axis=0).astype(BF16)
        out = out + jnp.dot(a_off, v_b, preferred_element_type=F32)

        diag = []
        for i in range(c // sub):
            sl = slice(i * sub, (i + 1) * sub)
            cum_i, q_i, k_i, v_i = cum[sl], q[sl], k[sl], v[sl]
            o_i = jnp.zeros((sub, LANE), F32)
            for s in range(sub):
                e = jnp.exp(jnp.minimum(cum_i - cum_i[s:s + 1, :], 0.0))
                w = jnp.where(row_s >= s, q_i * e * k_i[s:s + 1, :], 0.0)
                o_i = o_i + jnp.sum(w, axis=-1, keepdims=True) * v_i[s:s + 1, :]
            diag.append(o_i)
        out = out + jnp.concatenate(diag, axis=0)

        last = cum[c - 1:c, :]
        kdl = (k * jnp.exp(last - cum)).astype(BF16)
        state_ref[...] = jnp.exp(last) * state_t + jnp.dot(v.T.astype(BF16), kdl,
                                                            preferred_element_type=F32)

        ms = jnp.mean(out * out, axis=-1, keepdims=True)
        o_ref[pl.ds(r0, c), :] = (out * lax.rsqrt(ms + LN_EPS) * ng * _sigmoid(gr)).astype(o_ref.dtype)
        return carry

    lax.fori_loop(0, rows // c, chunk, 0)


def _hgrn_core(proj, lb, norm_g, *, batch, seq, rows=512):
    nh = N_HEADS
    t = batch * seq
    assert seq % rows == 0 and rows % HG_CHUNK == 0
    spb = seq // rows

    def head_spec(offset):
        return pl.BlockSpec((1, rows, LANE), lambda b, h, s: (offset + h, b * spb + s, 0))

    return pl.pallas_call(
        functools.partial(_hgrn_kernel, rows=rows),
        grid=(batch, nh, spb),
        in_specs=[pl.BlockSpec((1, 1, LANE), lambda b, h, s: (h, 0, 0)),
                  pl.BlockSpec((1, LANE), lambda b, h, s: (0, 0)),
                  head_spec(0), head_spec(nh), head_spec(2 * nh), head_spec(3 * nh)],
        out_specs=pl.BlockSpec((rows, LANE), lambda b, h, s: (b * spb + s, h)),
        out_shape=jax.ShapeDtypeStruct((t, nh * LANE), BF16),
        scratch_shapes=[pltpu.VMEM((LANE, LANE), F32)],
        compiler_params=_params("parallel", "parallel", "arbitrary"),
        name="hgrn2_core",
    )(lb.reshape(nh, 1, LANE), norm_g.reshape(1, LANE), proj, proj, proj, proj)


def _hgrn2_layer(h, w_in, lb_logits, norm_g, w_out, ln_g, ln_b, *, batch, seq, layer):
    lb = jnp.cumsum(jax.nn.softmax(lb_logits.astype(F32), axis=0), axis=0)[layer]
    (proj,) = _project_heads(h, w_in.astype(BF16), [4 * N_HEADS], [F32], seq=seq)
    o = _hgrn_core(proj, lb, norm_g, batch=batch, seq=seq)
    return _outproj_ln(o, w_out.astype(BF16), h, ln_g, ln_b)


def _rope_tables(seq, rot_dim):
    half = rot_dim // 2
    inv = ROPE_THETA ** (-jnp.arange(half, dtype=F32) / half)
    ang = jnp.arange(seq).astype(F32)[:, None] * inv[None, :]
    reps = LANE // rot_dim
    cos = jnp.tile(jnp.concatenate([jnp.cos(ang), jnp.cos(ang)], axis=1), (1, reps))
    sin = jnp.tile(jnp.concatenate([-jnp.sin(ang), jnp.sin(ang)], axis=1), (1, reps))
    return cos, sin


def _rope_chunk(val, cos, sin, rot_dim):
    half = rot_dim // 2
    if rot_dim == LANE:
        partner = pltpu.roll(val, half, 1)
    else:
        lane = lax.broadcasted_iota(jnp.int32, val.shape, 1)
        partner = jnp.where(lane % rot_dim < half,
                            pltpu.roll(val, LANE - half, 1), pltpu.roll(val, half, 1))
    return val * cos + partner * sin


def _online_softmax_step(s, v, m_ref, l_ref, acc_ref, mask=None):
    m_prev = m_ref[...]
    m_new = jnp.maximum(m_prev, jnp.max(s, axis=-1, keepdims=True))
    alpha = jnp.exp(m_prev - m_new)
    p = jnp.exp(s - m_new)
    if mask is not None:
        p = jnp.where(mask, p, 0.0)
    l_ref[...] = alpha * l_ref[...] + jnp.sum(p, axis=-1, keepdims=True)
    acc_ref[...] = alpha * acc_ref[...] + jnp.dot(p.astype(BF16), v, preferred_element_type=F32)
    m_ref[...] = m_new


def _diff_attn_kernel(lmb_ref, ng_ref, q_ref, k_ref, v_ref, o_ref,
                      m1, l1, a1, m2, l2, a2, *, tile, out_scale):
    qi = pl.program_id(2)
    q = q_ref[0]
    lane = lax.broadcasted_iota(jnp.int32, q.shape, 1)
    zero = jnp.zeros_like(q)
    qa = jnp.where(lane < DA_HALF, q, zero)
    qb = jnp.where(lane >= DA_HALF, q, zero)
    for m_ref, l_ref, a_ref in ((m1, l1, a1), (m2, l2, a2)):
        m_ref[...] = jnp.full_like(m_ref, NEG)
        l_ref[...] = jnp.zeros_like(l_ref)
        a_ref[...] = jnp.zeros_like(a_ref)

    def step(kj, mask):
        k0 = pl.multiple_of(kj * tile, tile)
        k = k_ref[0, pl.ds(k0, tile), :]
        v = v_ref[0, pl.ds(k0, tile), :]
        s1 = lax.dot_general(qa, k, _NT, preferred_element_type=F32)
        s2 = lax.dot_general(qb, k, _NT, preferred_element_type=F32)
        if mask is not None:
            s1 = jnp.where(mask, s1, NEG)
            s2 = jnp.where(mask, s2, NEG)
        _online_softmax_step(s1, v, m1, l1, a1, mask)
        _online_softmax_step(s2, v, m2, l2, a2, mask)

    def body(kj, carry):
        step(kj, None)
        return carry

    lax.fori_loop(0, qi, body, 0)
    row = lax.broadcasted_iota(jnp.int32, (tile, tile), 0)
    col = lax.broadcasted_iota(jnp.int32, (tile, tile), 1)
    step(qi, row >= col)

    o = a1[...] / l1[...] - lmb_ref[0] * (a2[...] / l2[...])
    ms = jnp.mean(o * o, axis=-1, keepdims=True)
    o_ref[...] = (o * lax.rsqrt(ms + LN_EPS) * ng_ref[...] * out_scale).astype(o_ref.dtype)


def _diff_attn_core(qkv, lmb, norm_g, *, batch, seq, lam_init, tile=256):
    nh = N_HEADS
    t = batch * seq
    assert seq % tile == 0
    nq = seq // tile
    stat = pltpu.VMEM((tile, 1), F32)
    acc = pltpu.VMEM((tile, LANE), F32)
    return pl.pallas_call(
        functools.partial(_diff_attn_kernel, tile=tile, out_scale=1.0 - lam_init),
        grid=(batch, nh, nq),
        in_specs=[pl.BlockSpec(memory_space=pltpu.SMEM),
                  pl.BlockSpec((1, LANE), lambda b, h, i: (0, 0)),
                  pl.BlockSpec((1, tile, LANE), lambda b, h, i: (h, b * nq + i, 0)),
                  pl.BlockSpec((1, seq, LANE), lambda b, h, i: (nh + h, b, 0)),
                  pl.BlockSpec((1, seq, LANE), lambda b, h, i: (2 * nh + h, b, 0))],
        out_specs=pl.BlockSpec((tile, LANE), lambda b, h, i: (b * nq + i, h)),
        out_shape=jax.ShapeDtypeStruct((t, nh * LANE), BF16),
        scratch_shapes=[stat, stat, acc, stat, stat, acc],
        compiler_params=_params("parallel", "parallel", "arbitrary"),
        name="diff_attention",
    )(lmb.reshape(1), norm_g.reshape(1, LANE), qkv, qkv, qkv)


def _diff_attention_layer(h, w_in, lam, norm_g, w_out, ln_g, ln_b, w_router, *, batch, seq, layer):
    nh = N_HEADS
    lam_init = 0.8 - 0.6 * math.exp(-0.3 * layer)
    lf = lam.astype(F32)
    lmb = jnp.exp(jnp.sum(lf[0] * lf[1])) - jnp.exp(jnp.sum(lf[2] * lf[3])) + lam_init
    cos, sin = _rope_tables(seq, DA_HALF)
    q_scale = DA_HALF ** -0.5

    def epilogue(c, val, aux):
        if c < 2 * nh:
            val = _rope_chunk(val, aux[0], aux[1], DA_HALF)
        if c < nh:
            val = val * q_scale
        return val

    (qkv,) = _project_heads(h, w_in.astype(BF16), [3 * nh], [BF16], seq=seq, aux=(cos, sin), epilogue=epilogue)
    o = _diff_attn_core(qkv, lmb, norm_g, batch=batch, seq=seq, lam_init=lam_init)
    return _outproj_ln(o, w_out.astype(BF16), h, ln_g, ln_b, w_router)


def _stick_kernel(q_ref, k_ref, v_ref, o_ref, acc_ref, run_ref, *, tile, scale):
    qi = pl.program_id(2)
    q = q_ref[0]
    acc_ref[...] = jnp.zeros_like(acc_ref)
    run_ref[...] = jnp.zeros_like(run_ref)
    row = lax.broadcasted_iota(jnp.int32, (tile, tile), 0)
    col = lax.broadcasted_iota(jnp.int32, (tile, tile), 1)
    later = (row > col).astype(BF16)

    def step(kj, strict):
        k0 = pl.multiple_of(kj * tile, tile)
        k = k_ref[0, pl.ds(k0, tile), :]
        v = v_ref[0, pl.ds(k0, tile), :]
        z = lax.dot_general(q, k, _NT, preferred_element_type=F32) * scale
        soft = jnp.log1p(jnp.exp(-jnp.abs(z)))
        log_beta = jnp.minimum(z, 0.0) - soft
        log_keep = log_beta - z
        if strict is not None:
            log_keep = jnp.where(strict, log_keep, 0.0)
        keep_hi = log_keep.astype(BF16)
        keep_lo = (log_keep - keep_hi.astype(F32)).astype(BF16)
        after = (jnp.dot(keep_hi, later, preferred_element_type=F32)
                 + jnp.dot(keep_lo, later, preferred_element_type=F32) + run_ref[...])
        a = jnp.exp(log_beta + after)
        if strict is not None:
            a = jnp.where(strict, a, 0.0)
        acc_ref[...] += jnp.dot(a.astype(BF16), v, preferred_element_type=F32)
        run_ref[...] += jnp.sum(log_keep, axis=-1, keepdims=True)

    step(qi, row > col)

    def body(j, carry):
        step(qi - 1 - j, None)
        return carry

    lax.fori_loop(0, qi, body, 0)
    o_ref[...] = acc_ref[...].astype(o_ref.dtype)


def _stick_core(qkv, *, batch, seq, tile=256):
    nh = N_HEADS
    t = batch * seq
    assert seq % tile == 0
    nq = seq // tile
    return pl.pallas_call(
        functools.partial(_stick_kernel, tile=tile, scale=HEAD_DIM ** -0.5),
        grid=(batch, nh, nq),
        in_specs=[pl.BlockSpec((1, tile, LANE), lambda b, h, i: (h, b * nq + i, 0)),
                  pl.BlockSpec((1, seq, LANE), lambda b, h, i: (nh + h, b, 0)),
                  pl.BlockSpec((1, seq, LANE), lambda b, h, i: (2 * nh + h, b, 0))],
        out_specs=pl.BlockSpec((tile, LANE), lambda b, h, i: (b * nq + i, h)),
        out_shape=jax.ShapeDtypeStruct((t, nh * LANE), BF16),
        scratch_shapes=[pltpu.VMEM((tile, LANE), F32), pltpu.VMEM((tile, 1), F32)],
        compiler_params=_params("parallel", "parallel", "arbitrary"),
        name="stick_breaking",
    )(qkv, qkv, qkv)


def _stick_breaking_layer(h, w_in, w_out, ln_g, ln_b, w_router, *, batch, seq):
    (qkv,) = _project_heads(h, w_in.astype(BF16), [3 * N_HEADS], [BF16], seq=seq)
    o = _stick_core(qkv, batch=batch, seq=seq)
    return _outproj_ln(o, w_out.astype(BF16), h, ln_g, ln_b, w_router)


def _nsa_compress_kernel(x_ref, w1a_ref, w1b_ref, pe_ref, w2_ref, o_ref):
    x = x_ref[0]
    first = jnp.dot(x, w1a_ref[0, 0], preferred_element_type=F32)
    second = jnp.dot(x, w1b_ref[0, 0], preferred_element_type=F32)
    n16 = x.shape[0]
    second = pltpu.roll(second, n16 - 1, 0)
    pe = jnp.broadcast_to(pe_ref[0], (8, pe_ref.shape[2])).astype(BF16)
    bias = (jnp.dot(pe[:, :x.shape[1]], w1a_ref[0, 0], preferred_element_type=F32)
            + jnp.dot(pe[:, x.shape[1]:], w1b_ref[0, 0], preferred_element_type=F32))
    pre = first + second + bias[0:1, :]
    hid = jax.nn.gelu(pre)
    o_ref[0] = jnp.dot(hid.astype(BF16), w2_ref[0], preferred_element_type=F32).astype(o_ref.dtype)


def _nsa_compress(proj, pe, w1, w2, *, batch, seq, first_chunk):
    g_n = NSA_GROUPS
    n16 = seq // CMP_STRIDE
    half = CMP_STRIDE * LANE
    nch = proj.shape[0]
    x16 = proj.reshape(nch, batch * n16, half)
    w1r = w1.astype(BF16).reshape(2, 2, half, LANE)
    per = pe.astype(F32).reshape(2, 1, 2 * half)
    return pl.pallas_call(
        _nsa_compress_kernel,
        grid=(2, batch, g_n),
        in_specs=[pl.BlockSpec((1, n16, half), lambda j, b, g: (first_chunk + j * g_n + g, b, 0)),
                  pl.BlockSpec((1, 1, half, LANE), lambda j, b, g: (j, 0, 0, 0)),
                  pl.BlockSpec((1, 1, half, LANE), lambda j, b, g: (j, 1, 0, 0)),
                  pl.BlockSpec((1, 1, 2 * half), lambda j, b, g: (j, 0, 0)),
                  pl.BlockSpec((1, LANE, LANE), lambda j, b, g: (j, 0, 0))],
        out_specs=pl.BlockSpec((1, n16, LANE), lambda j, b, g: ((j * batch + b) * g_n + g, 0, 0)),
        out_shape=jax.ShapeDtypeStruct((2 * batch * g_n, n16, LANE), BF16),
        compiler_params=_params("parallel", "parallel", "parallel"),
        name="nsa_compress",
    )(x16, w1r, w1r, per, w2.astype(BF16))


def _nsa_cmp_select_kernel(q_ref, kc_ref, vc_ref, c2s_ref, o_ref, sel_ref, *, tq, scale):
    qi = pl.program_id(2)
    kc = kc_ref[0]
    vc = vc_ref[0]
    ncp = kc.shape[0]
    tpos = qi * tq + lax.broadcasted_iota(jnp.int32, (tq, 1), 0)
    cend = lax.broadcasted_iota(jnp.int32, (1, ncp), 1) * CMP_STRIDE + (CMP_BLOCK - 1)
    valid = cend <= tpos
    psum = jnp.zeros((tq, ncp), F32)
    for hh in range(NSA_HPG):
        s = lax.dot_general(q_ref[hh], kc, _NT, preferred_element_type=F32) * scale
        s = jnp.where(valid, s, NEG)
        m = jnp.max(s, axis=-1, keepdims=True)
        e = jnp.where(valid, jnp.exp(s - m), 0.0)
        l = jnp.sum(e, axis=-1, keepdims=True)
        p = e / jnp.where(l > 0.0, l, 1.0)
        o_ref[:, hh * LANE:(hh + 1) * LANE] = jnp.dot(p.astype(BF16), vc, preferred_element_type=F32)
        psum = psum + p
    imp = jnp.dot(psum.astype(BF16), c2s_ref[...], preferred_element_type=F32)
    blk = lax.broadcasted_iota(jnp.int32, imp.shape, 1)
    cur = tpos >> (SLC_BLOCK.bit_length() - 1)
    forced = (blk == 0) | (blk == cur) | (blk == cur - 1)
    imp = jnp.where(forced, FORCE, imp)
    imp = jnp.where(blk <= cur, imp, NEG)
    blk_f = blk.astype(F32)
    sel = jnp.zeros(imp.shape, jnp.bool_)
    x = imp
    for _ in range(SLC_TOPK):
        m = jnp.max(x, axis=-1, keepdims=True)
        idx = jnp.min(jnp.where(x == m, blk_f, float(LANE)), axis=-1, keepdims=True)
        hit = blk_f == idx
        sel = sel | (hit & (m > 0.5 * NEG))
        x = jnp.where(hit, -jnp.inf, x)
    sel_ref[0] = jnp.where(sel, 0.0, NEG).astype(sel_ref.dtype)


def _nsa_cmp_select(proj, kvc, c2s, *, batch, seq, tq=256):
    g_n, hpg = NSA_GROUPS, NSA_HPG
    t = batch * seq
    nq = seq // tq
    n16 = seq // CMP_STRIDE
    return pl.pallas_call(
        functools.partial(_nsa_cmp_select_kernel, tq=tq, scale=HEAD_DIM ** -0.5),
        grid=(batch, g_n, nq),
        in_specs=[pl.BlockSpec((hpg, tq, LANE), lambda b, g, i: (g, b * nq + i, 0)),
                  pl.BlockSpec((1, n16, LANE), lambda b, g, i: (b * g_n + g, 0, 0)),
                  pl.BlockSpec((1, n16, LANE), lambda b, g, i: ((batch + b) * g_n + g, 0, 0)),
                  pl.BlockSpec((n16, LANE), lambda b, g, i: (0, 0))],
        out_specs=[pl.BlockSpec((tq, hpg * LANE), lambda b, g, i: (b * nq + i, g)),
                   pl.BlockSpec((1, tq, LANE), lambda b, g, i: (g, b * nq + i, 0))],
        out_shape=[jax.ShapeDtypeStruct((t, g_n * hpg * LANE), F32),
                   jax.ShapeDtypeStruct((g_n, t, LANE), BF16)],
        compiler_params=_params("parallel", "parallel", "parallel"),
        name="nsa_compressed_select",
    )(proj, kvc, kvc, c2s)


def _nsa_selected_kernel(q_ref, sel_ref, k_ref, v_ref, et_ref, o_ref, qa_ref, m_ref, l_ref, acc_ref,
                         *, tq, tk, scale):
    qi = pl.program_id(2)
    hpg = NSA_HPG
    for hh in range(hpg):
        qa_ref[hh * tq:(hh + 1) * tq, :LANE] = q_ref[hh]
        qa_ref[hh * tq:(hh + 1) * tq, LANE:] = sel_ref[0]
    qa = qa_ref[...]
    m_ref[...] = jnp.full_like(m_ref, NEG)
    l_ref[...] = jnp.zeros_like(l_ref)
    acc_ref[...] = jnp.zeros_like(acc_ref)

    def step(kj, mask):
        k0 = pl.multiple_of(kj * tk, tk)
        ka = jnp.concatenate([k_ref[0, pl.ds(k0, tk), :], et_ref[pl.ds(k0, tk), :]], axis=1)
        v = v_ref[0, pl.ds(k0, tk), :]
        s = lax.dot_general(qa, ka, _NT, preferred_element_type=F32) * scale
        if mask is not None:
            s = jnp.where(mask, s, NEG)
        _online_softmax_step(s, v, m_ref, l_ref, acc_ref, mask)

    def body(kj, carry):
        step(kj, None)
        return carry

    last = (qi * tq + tq - 1) // tk
    lax.fori_loop(0, last, body, 0)
    trow = qi * tq + (lax.broadcasted_iota(jnp.int32, (hpg * tq, tk), 0) & (tq - 1))
    kcol = last * tk + lax.broadcasted_iota(jnp.int32, (hpg * tq, tk), 1)
    step(last, kcol <= trow)
    for hh in range(hpg):
        rows = slice(hh * tq, (hh + 1) * tq)
        o_ref[:, hh * LANE:(hh + 1) * LANE] = acc_ref[rows, :] / l_ref[rows, :]


def _nsa_selected(proj, sel, et, *, batch, seq, q_chunk, k_chunk, v_chunk, tq=128, tk=256):
    g_n, hpg = NSA_GROUPS, NSA_HPG
    t = batch * seq
    nq = seq // tq
    assert tq & (tq - 1) == 0 and seq % tk == 0 and q_chunk % hpg == 0
    return pl.pallas_call(
        functools.partial(_nsa_selected_kernel, tq=tq, tk=tk, scale=HEAD_DIM ** -0.5),
        grid=(batch, g_n, nq),
        in_specs=[pl.BlockSpec((hpg, tq, LANE), lambda b, g, i: (q_chunk // hpg + g, b * nq + i, 0)),
                  pl.BlockSpec((1, tq, LANE), lambda b, g, i: (g, b * nq + i, 0)),
                  pl.BlockSpec((1, seq, LANE), lambda b, g, i: (k_chunk + g, b, 0)),
                  pl.BlockSpec((1, seq, LANE), lambda b, g, i: (v_chunk + g, b, 0)),
                  pl.BlockSpec((seq, LANE), lambda b, g, i: (0, 0))],
        out_specs=pl.BlockSpec((tq, hpg * LANE), lambda b, g, i: (b * nq + i, g)),
        out_shape=jax.ShapeDtypeStruct((t, g_n * hpg * LANE), F32),
        scratch_shapes=[pltpu.VMEM((hpg * tq, 2 * LANE), BF16), pltpu.VMEM((hpg * tq, 1), F32),
                        pltpu.VMEM((hpg * tq, 1), F32), pltpu.VMEM((hpg * tq, LANE), F32)],
        compiler_params=_params("parallel", "parallel", "arbitrary"),
        name="nsa_selected",
    )(proj, sel, proj, proj, et)


def _nsa_window_kernel(q_ref, k_ref, v_ref, o_ref, q2_ref, m_ref, l_ref, acc_ref, *, tile, scale):
    qi = pl.program_id(2)
    hpg = NSA_HPG
    for hh in range(hpg):
        q2_ref[hh * tile:(hh + 1) * tile, :] = q_ref[hh]
    q = q2_ref[...]
    m_ref[...] = jnp.full_like(m_ref, NEG)
    l_ref[...] = jnp.zeros_like(l_ref)
    acc_ref[...] = jnp.zeros_like(acc_ref)
    trow = lax.broadcasted_iota(jnp.int32, (hpg * tile, tile), 0) & (tile - 1)
    kcol = lax.broadcasted_iota(jnp.int32, (hpg * tile, tile), 1)

    def step(kj, mask):
        k0 = pl.multiple_of(kj * tile, tile)
        s = lax.dot_general(q, k_ref[0, pl.ds(k0, tile), :], _NT, preferred_element_type=F32) * scale
        if mask is not None:
            s = jnp.where(mask, s, NEG)
        _online_softmax_step(s, v_ref[0, pl.ds(k0, tile), :], m_ref, l_ref, acc_ref, mask)

    n_back = WINDOW // tile
    for back in range(n_back, 0, -1):
        mask = (kcol > trow) if back == n_back else None

        @pl.when(qi >= back)
        def _(back=back, mask=mask):
            step(qi - back, mask)

    step(qi, kcol <= trow)
    for hh in range(hpg):
        rows = slice(hh * tile, (hh + 1) * tile)
        o_ref[:, hh * LANE:(hh + 1) * LANE] = acc_ref[rows, :] / l_ref[rows, :]


def _nsa_window(proj, *, batch, seq, q_chunk, k_chunk, v_chunk, tile=256):
    g_n, hpg = NSA_GROUPS, NSA_HPG
    t = batch * seq
    nq = seq // tile
    assert tile & (tile - 1) == 0 and WINDOW % tile == 0 and q_chunk % hpg == 0
    return pl.pallas_call(
        functools.partial(_nsa_window_kernel, tile=tile, scale=HEAD_DIM ** -0.5),
        grid=(batch, g_n, nq),
        in_specs=[pl.BlockSpec((hpg, tile, LANE), lambda b, g, i: (q_chunk // hpg + g, b * nq + i, 0)),
                  pl.BlockSpec((1, seq, LANE), lambda b, g, i: (k_chunk + g, b, 0)),
                  pl.BlockSpec((1, seq, LANE), lambda b, g, i: (v_chunk + g, b, 0))],
        out_specs=pl.BlockSpec((tile, hpg * LANE), lambda b, g, i: (b * nq + i, g)),
        out_shape=jax.ShapeDtypeStruct((t, g_n * hpg * LANE), F32),
        scratch_shapes=[pltpu.VMEM((hpg * tile, LANE), BF16), pltpu.VMEM((hpg * tile, 1), F32),
                        pltpu.VMEM((hpg * tile, 1), F32), pltpu.VMEM((hpg * tile, LANE), F32)],
        compiler_params=_params("parallel", "parallel", "arbitrary"),
        name="nsa_window",
    )(proj, proj, proj)


def _nsa_out_kernel(oc_ref, os_ref, ow_ref, gate_ref, w_ref, h_ref, g_ref, b_ref, out_ref):
    gates = gate_ref[...]
    parts = []
    for hh in range(N_HEADS):
        cols = slice(hh * LANE, (hh + 1) * LANE)
        parts.append(gates[:, 3 * hh:3 * hh + 1] * oc_ref[:, cols]
                     + gates[:, 3 * hh + 1:3 * hh + 2] * os_ref[:, cols]
                     + gates[:, 3 * hh + 2:3 * hh + 3] * ow_ref[:, cols])
    o = jnp.concatenate(parts, axis=1).astype(BF16)
    y = jnp.dot(o, w_ref[...], preferred_element_type=F32)
    out_ref[...] = _layer_norm(DEEPNORM_ALPHA * h_ref[...] + y, g_ref[...], b_ref[...])


def _nsa_out_ln(o_cmp, o_slc, o_win, gates, w, h, g, b, *, tm=256):
    t, d = h.shape
    row = lambda i: (i, 0)
    fixed = lambda i: (0, 0)
    return pl.pallas_call(
        _nsa_out_kernel,
        grid=(t // tm,),
        in_specs=[pl.BlockSpec((tm, d), row), pl.BlockSpec((tm, d), row), pl.BlockSpec((tm, d), row),
                  pl.BlockSpec((tm, LANE), row), pl.BlockSpec((d, d), fixed), pl.BlockSpec((tm, d), row),
                  pl.BlockSpec((1, d), fixed), pl.BlockSpec((1, d), fixed)],
        out_specs=pl.BlockSpec((tm, d), row),
        out_shape=jax.ShapeDtypeStruct((t, d), F32),
        compiler_params=_params("parallel"),
        name="nsa_out_ln",
    )(o_cmp, o_slc, o_win, gates, w, h, g.reshape(1, d), b.reshape(1, d))


def _nsa_layer(h, w_in, cmp_pe, cmp_w1, cmp_w2, w_out, ln_g, ln_b, *, batch, seq):
    nh, g_n, d = N_HEADS, NSA_GROUPS, HEAD_DIM
    dm = nh * d
    n_slc = seq // SLC_BLOCK
    n16 = seq // CMP_STRIDE
    n_cmp = (seq - CMP_BLOCK) // CMP_STRIDE + 1
    assert n_slc <= LANE and n_cmp <= n16
    wq = w_in[:, :dm]
    wkv = w_in[:, dm:dm + 6 * g_n * d]
    wg = jnp.pad(w_in[:, dm + 6 * g_n * d:], ((0, 0), (0, LANE - 3 * nh)))
    w_all = jnp.concatenate([wq, wq, wkv, wg], axis=1).astype(BF16)
    kv0 = 2 * nh
    rotated = set(range(nh, 2 * nh)) | {kv0 + 2 * g_n + g for g in range(g_n)} | {kv0 + 4 * g_n + g for g in range(g_n)}
    cos, sin = _rope_tables(seq, d)

    def epilogue(c, val, aux):
        if c in rotated:
            val = _rope_chunk(val, aux[0], aux[1], d)
        if c == kv0 + 6 * g_n:
            val = _sigmoid(val)
        return val

    proj, gates = _project_heads(h, w_all, [kv0 + 6 * g_n, 1], [BF16, F32], seq=seq,
                                 aux=(cos, sin), epilogue=epilogue)
    kvc = _nsa_compress(proj, cmp_pe, cmp_w1, cmp_w2, batch=batch, seq=seq, first_chunk=kv0)
    cs = jnp.arange(n16) * CMP_STRIDE
    ss = jnp.arange(LANE) * SLC_BLOCK
    ov = jnp.clip(jnp.minimum(cs[:, None] + CMP_BLOCK, ss[None, :] + SLC_BLOCK)
                  - jnp.maximum(cs[:, None], ss[None, :]), 0, None) / CMP_BLOCK
    ov = jnp.where((jnp.arange(n16)[:, None] < n_cmp) & (jnp.arange(LANE)[None, :] < n_slc), ov, 0.0)
    c2s = ov.astype(BF16)
    et = (jnp.arange(seq)[:, None] // SLC_BLOCK == jnp.arange(LANE)[None, :]).astype(BF16)
    o_cmp, sel = _nsa_cmp_select(proj, kvc, c2s, batch=batch, seq=seq)
    o_slc = _nsa_selected(proj, sel, et, batch=batch, seq=seq, q_chunk=nh,
                          k_chunk=kv0 + 2 * g_n, v_chunk=kv0 + 3 * g_n)
    o_win = _nsa_window(proj, batch=batch, seq=seq, q_chunk=nh,
                        k_chunk=kv0 + 4 * g_n, v_chunk=kv0 + 5 * g_n)
    return _nsa_out_ln(o_cmp, o_slc, o_win, gates[0], w_out.astype(BF16), h, ln_g, ln_b)


def _moe_expert_kernel(te_ref, nu_ref, x_ref, wg_ref, wu_ref, wd_ref, out_ref, acc_ref):
    i = pl.program_id(0)
    k = pl.program_id(1)
    last = pl.num_programs(1) - 1
    used = i < nu_ref[0]

    @pl.when(used)
    def _():
        @pl.when(k == 0)
        def _():
            acc_ref[...] = jnp.zeros_like(acc_ref)

        x = x_ref[...]
        gate = jnp.dot(x, wg_ref[0], preferred_element_type=F32)
        up = jnp.dot(x, wu_ref[0], preferred_element_type=F32)
        act = (gate * _sigmoid(gate) * up).astype(BF16)
        acc_ref[...] += jnp.dot(act, wd_ref[0], preferred_element_type=F32)

        @pl.when(k == last)
        def _():
            out_ref[...] = acc_ref[...]

    @pl.when(jnp.logical_not(used) & (k == last))
    def _():
        out_ref[...] = jnp.zeros_like(out_ref)


def _moe_experts(xs, tile_expert, n_used, wg, wu, wd, *, tm, tf=512):
    n_slots, d = xs.shape
    ffe = wg.shape[2]
    assert n_slots % tm == 0 and ffe % tf == 0
    grid_spec = pltpu.PrefetchScalarGridSpec(
        num_scalar_prefetch=2,
        grid=(n_slots // tm, ffe // tf),
        in_specs=[pl.BlockSpec((tm, d), lambda i, k, te, nu: (i, 0)),
                  pl.BlockSpec((1, d, tf), lambda i, k, te, nu: (te[i], 0, k)),
                  pl.BlockSpec((1, d, tf), lambda i, k, te, nu: (te[i], 0, k)),
                  pl.BlockSpec((1, tf, d), lambda i, k, te, nu: (te[i], k, 0))],
        out_specs=pl.BlockSpec((tm, d), lambda i, k, te, nu: (i, 0)),
        scratch_shapes=[pltpu.VMEM((tm, d), F32)])
    return pl.pallas_call(
        _moe_expert_kernel,
        grid_spec=grid_spec,
        out_shape=jax.ShapeDtypeStruct((n_slots, d), F32),
        compiler_params=_params("parallel", "arbitrary"),
        name="moe_experts",
    )(tile_expert, n_used, xs, wg, wu, wd)


def _moe_combine_kernel(y_ref, gate_ref, h_ref, g_ref, b_ref, out_ref):
    d = h_ref.shape[1]
    gates = gate_ref[...]
    y = gates[:, 0:1] * y_ref[:, :d] + gates[:, 1:2] * y_ref[:, d:]
    out_ref[...] = _layer_norm(DEEPNORM_ALPHA * h_ref[...] + y, g_ref[...], b_ref[...])


def _moe_combine_ln(y_pair, gates, h, g, b, *, tm=512):
    t, d = h.shape
    row = lambda i: (i, 0)
    fixed = lambda i: (0, 0)
    return pl.pallas_call(
        _moe_combine_kernel,
        grid=(t // tm,),
        in_specs=[pl.BlockSpec((tm, TOP_K * d), row), pl.BlockSpec((tm, TOP_K), row), pl.BlockSpec((tm, d), row),
                  pl.BlockSpec((1, d), fixed), pl.BlockSpec((1, d), fixed)],
        out_specs=pl.BlockSpec((tm, d), row),
        out_shape=jax.ShapeDtypeStruct((t, d), F32),
        compiler_params=_params("parallel"),
        name="moe_combine_ln",
    )(y_pair, gates, h, g.reshape(1, d), b.reshape(1, d))


def _moe_layer(h, h_b, logits, wg, wu, wd, ln_g, ln_b, *, tm=512):
    t, d = h.shape
    ne = N_EXPERTS
    top_val, top_idx = lax.top_k(logits[:, :ne], TOP_K)
    gates = jax.nn.softmax(top_val, axis=-1)
    n_assign = t * TOP_K
    flat_e = top_idx.reshape(n_assign)
    flat_tok = jnp.repeat(jnp.arange(t, dtype=jnp.int32), TOP_K)
    onehot = (flat_e[:, None] == jnp.arange(ne, dtype=flat_e.dtype)[None, :]).astype(jnp.int32)
    before = jnp.cumsum(onehot, axis=0) - onehot
    rank = jnp.sum(before * onehot, axis=1)
    counts = jnp.sum(onehot, axis=0)
    padded = (counts + tm - 1) // tm * tm
    pad_end = jnp.cumsum(padded)
    pad_start = pad_end - padded
    dest = (pad_start[flat_e] + rank).astype(jnp.int32)
    n_tiles = -(-n_assign // tm) + ne
    slot_tok = jnp.zeros((n_tiles * tm,), jnp.int32).at[dest].set(flat_tok)
    tile_expert = jnp.minimum(jnp.searchsorted(pad_end, jnp.arange(n_tiles) * tm, side='right'),
                              ne - 1).astype(jnp.int32)
    n_used = (pad_end[-1] // tm).astype(jnp.int32).reshape(1)
    xs = h_b[slot_tok]
    ys = _moe_experts(xs, tile_expert, n_used, wg, wu, wd, tm=tm)
    y_pair = ys[dest].reshape(t, TOP_K * d)
    return _moe_combine_ln(y_pair, gates, h, ln_g, ln_b)


def kernel(x, hg_w_in, hg_lb, hg_norm_g, hg_w_out, da_w_in, da_lam, da_norm_g, da_w_out, nsa_w_in, nsa_cmp_pe, nsa_cmp_w1, nsa_cmp_w2, nsa_w_out, sb_w_in, sb_w_out, ffn_w_gate, ffn_w_up, ffn_w_down, moe_w_router, moe_w_gate, moe_w_up, moe_w_down, ln_g, ln_b):
    batch, seq, d = x.shape
    h = x.reshape(batch * seq, d)

    def router(j):
        return jnp.pad(moe_w_router[j], ((0, 0), (0, LANE - N_EXPERTS))).astype(BF16)

    def dense_ffn(h, j, layer):
        return _ffn_ln(h, ffn_w_gate[j].astype(BF16), ffn_w_up[j].astype(BF16), ffn_w_down[j].astype(BF16),
                       ln_g[layer, 1], ln_b[layer, 1])

    def expert_ffn(h, h_b, logits, j, layer):
        return _moe_layer(h, h_b, logits, moe_w_gate[j].astype(BF16), moe_w_up[j].astype(BF16),
                          moe_w_down[j].astype(BF16), ln_g[layer, 1], ln_b[layer, 1])

    h = _hgrn2_layer(h, hg_w_in, hg_lb, hg_norm_g, hg_w_out, ln_g[0, 0], ln_b[0, 0],
                     batch=batch, seq=seq, layer=0)
    h = dense_ffn(h, 0, 0)
    h, h_b, logits = _diff_attention_layer(h, da_w_in, da_lam, da_norm_g, da_w_out, ln_g[1, 0], ln_b[1, 0],
                                           router(0), batch=batch, seq=seq, layer=1)
    h = expert_ffn(h, h_b, logits, 0, 1)
    h = _nsa_layer(h, nsa_w_in, nsa_cmp_pe, nsa_cmp_w1, nsa_cmp_w2, nsa_w_out, ln_g[2, 0], ln_b[2, 0],
                   batch=batch, seq=seq)
    h = dense_ffn(h, 1, 2)
    h, h_b, logits = _stick_breaking_layer(h, sb_w_in, sb_w_out, ln_g[3, 0], ln_b[3, 0], router(1),
                                           batch=batch, seq=seq)
    h = expert_ffn(h, h_b, logits, 1, 3)
    return h.reshape(batch, seq, d)
```

```python
import functools
import math

import jax
import jax.numpy as jnp
from jax import lax
from jax.experimental import pallas as pl
from jax.experimental.pallas import tpu as pltpu

F32 = jnp.float32
BF16 = jnp.bfloat16

LANE = 128
VMEM_LIMIT_BYTES = 56 * 1024 * 1024

N_HEADS = 8
HEAD_DIM = 128
ROPE_THETA = 10000.0
HG_CHUNK = 64
HG_SUB = 16
DA_HALF = 64
NSA_GROUPS = 2
NSA_HPG = 4
CMP_BLOCK = 32
CMP_STRIDE = 16
SLC_BLOCK = 64
SLC_TOPK = 16
WINDOW = 512
N_EXPERTS = 8
TOP_K = 2
LN_EPS = 1e-5
NEG = -1e30
FORCE = 1e9
DEPTH = 4
DEEPNORM_ALPHA = (2 * DEPTH) ** 0.25

_NT = (((1,), (1,)), ((), ()))

def _params(*semantics):
    return pltpu.CompilerParams(dimension_semantics=semantics, vmem_limit_bytes=VMEM_LIMIT_BYTES)


def _sigmoid(x):
    return 1.0 / (1.0 + jnp.exp(-x))


def _layer_norm(z, g, b):
    mu = jnp.mean(z, axis=-1, keepdims=True)
    zc = z - mu
    var = jnp.mean(zc * zc, axis=-1, keepdims=True)
    return zc * lax.rsqrt(var + LN_EPS) * g + b


def _proj_kernel(*refs, n_aux, groups, epilogue):
    x_ref, w_ref = refs[0], refs[1]
    aux_refs = refs[2:2 + n_aux]
    out_refs = refs[2 + n_aux:]
    x = x_ref[...].astype(BF16)
    aux = [r[...] for r in aux_refs]
    c = 0
    for out_ref, n in zip(out_refs, groups):
        for local in range(n):
            val = jnp.dot(x, w_ref[:, c * LANE:(c + 1) * LANE], preferred_element_type=F32)
            if epilogue is not None:
                val = epilogue(c, val, aux)
            out_ref[local] = val.astype(out_ref.dtype)
            c += 1


def _project_heads(x, w, groups, dtypes, *, seq, aux=(), epilogue=None, tm=512):
    t, d = x.shape
    n_chunks = sum(groups)
    assert w.shape == (d, n_chunks * LANE) and t % tm == 0 and seq % tm == 0
    tiles_per_seq = seq // tm
    in_specs = [pl.BlockSpec((tm, d), lambda i: (i, 0)),
                pl.BlockSpec((d, n_chunks * LANE), lambda i: (0, 0))]
    for a in aux:
        assert a.shape == (seq, LANE)
        in_specs.append(pl.BlockSpec((tm, LANE), lambda i: (i % tiles_per_seq, 0)))
    out_shape = [jax.ShapeDtypeStruct((n, t, LANE), dt) for n, dt in zip(groups, dtypes)]
    out_specs = [pl.BlockSpec((n, tm, LANE), lambda i: (0, i, 0)) for n in groups]
    return pl.pallas_call(
        functools.partial(_proj_kernel, n_aux=len(aux), groups=tuple(groups), epilogue=epilogue),
        grid=(t // tm,),
        in_specs=in_specs,
        out_specs=out_specs,
        out_shape=out_shape,
        compiler_params=_params("parallel"),
        name="project_heads",
    )(x, w, *aux)


def _outproj_ln_kernel(*refs, with_router):
    if with_router:
        o_ref, w_ref, h_ref, g_ref, b_ref, wr_ref, out_ref, outb_ref, logit_ref = refs
    else:
        o_ref, w_ref, h_ref, g_ref, b_ref, out_ref = refs
    y = jnp.dot(o_ref[...], w_ref[...], preferred_element_type=F32)
    hn = _layer_norm(DEEPNORM_ALPHA * h_ref[...] + y, g_ref[...], b_ref[...])
    out_ref[...] = hn
    if with_router:
        hb = hn.astype(BF16)
        outb_ref[...] = hb
        logit_ref[...] = jnp.dot(hb, wr_ref[...], preferred_element_type=F32)


def _outproj_ln(o, w, h, g, b, w_router=None, *, tm=512):
    t, d = h.shape
    assert o.shape == (t, d) and t % tm == 0
    with_router = w_router is not None
    row = lambda i: (i, 0)
    fixed = lambda i: (0, 0)
    in_specs = [pl.BlockSpec((tm, d), row), pl.BlockSpec((d, d), fixed), pl.BlockSpec((tm, d), row),
                pl.BlockSpec((1, d), fixed), pl.BlockSpec((1, d), fixed)]
    args = [o, w, h, g.reshape(1, d), b.reshape(1, d)]
    out_shape = [jax.ShapeDtypeStruct((t, d), F32)]
    out_specs = [pl.BlockSpec((tm, d), row)]
    if with_router:
        in_specs.append(pl.BlockSpec((d, LANE), fixed))
        args.append(w_router)
        out_shape += [jax.ShapeDtypeStruct((t, d), BF16), jax.ShapeDtypeStruct((t, LANE), F32)]
        out_specs += [pl.BlockSpec((tm, d), row), pl.BlockSpec((tm, LANE), row)]
    res = pl.pallas_call(
        functools.partial(_outproj_ln_kernel, with_router=with_router),
        grid=(t // tm,),
        in_specs=in_specs,
        out_specs=out_specs,
        out_shape=out_shape,
        compiler_params=_params("parallel"),
        name="outproj_ln",
    )(*args)
    return res if with_router else res[0]


def _ffn_kernel(h_ref, wg_ref, wu_ref, wd_ref, g_ref, b_ref, out_ref, xb_ref, acc_ref):
    k = pl.program_id(1)

    @pl.when(k == 0)
    def _():
        xb_ref[...] = h_ref[...].astype(BF16)
        acc_ref[...] = jnp.zeros_like(acc_ref)

    x = xb_ref[...]
    gate = jnp.dot(x, wg_ref[...], preferred_element_type=F32)
    up = jnp.dot(x, wu_ref[...], preferred_element_type=F32)
    act = (gate * _sigmoid(gate) * up).astype(BF16)
    acc_ref[...] += jnp.dot(act, wd_ref[...], preferred_element_type=F32)

    @pl.when(k == pl.num_programs(1) - 1)
    def _():
        out_ref[...] = _layer_norm(DEEPNORM_ALPHA * h_ref[...] + acc_ref[...], g_ref[...], b_ref[...])


def _ffn_ln(h, wg, wu, wd, g, b, *, tm=512, tf=256):
    t, d = h.shape
    ff = wg.shape[1]
    assert t % tm == 0 and ff % tf == 0
    return pl.pallas_call(
        _ffn_kernel,
        grid=(t // tm, ff // tf),
        in_specs=[pl.BlockSpec((tm, d), lambda i, k: (i, 0)),
                  pl.BlockSpec((d, tf), lambda i, k: (0, k)),
                  pl.BlockSpec((d, tf), lambda i, k: (0, k)),
                  pl.BlockSpec((tf, d), lambda i, k: (k, 0)),
                  pl.BlockSpec((1, d), lambda i, k: (0, 0)),
                  pl.BlockSpec((1, d), lambda i, k: (0, 0))],
        out_specs=pl.BlockSpec((tm, d), lambda i, k: (i, 0)),
        out_shape=jax.ShapeDtypeStruct((t, d), F32),
        scratch_shapes=[pltpu.VMEM((tm, d), BF16), pltpu.VMEM((tm, d), F32)],
        compiler_params=_params("parallel", "arbitrary"),
        name="swiglu_ln",
    )(h, wg, wu, wd, g.reshape(1, d), b.reshape(1, d))


def _hgrn_kernel(lb_ref, ng_ref, q_ref, f_ref, i_ref, g_ref, o_ref, state_ref, *, rows):
    c, sub = HG_CHUNK, HG_SUB

    @pl.when(pl.program_id(2) == 0)
    def _():
        state_ref[...] = jnp.zeros_like(state_ref)

    lb = lb_ref[0]
    ng = ng_ref[...]
    r_io = lax.broadcasted_iota(jnp.int32, (c, c), 0)
    c_io = lax.broadcasted_iota(jnp.int32, (c, c), 1)
    tri = (r_io >= c_io).astype(BF16)
    row_c = lax.broadcasted_iota(jnp.int32, (c, LANE), 0)
    row_s = lax.broadcasted_iota(jnp.int32, (sub, LANE), 0)

    def chunk(ci, carry):
        r0 = pl.multiple_of(ci * c, c)
        qr = q_ref[0, pl.ds(r0, c), :]
        fr = f_ref[0, pl.ds(r0, c), :]
        v = i_ref[0, pl.ds(r0, c), :]
        gr = g_ref[0, pl.ds(r0, c), :]
        forget = lb + (1.0 - lb) * _sigmoid(fr)
        lf = jnp.log(forget)
        k = 1.0 - forget
        q = qr * _sigmoid(qr)
        lf_hi = lf.astype(BF16)
        lf_lo = (lf - lf_hi.astype(F32)).astype(BF16)
        cum = (jnp.dot(tri, lf_hi, preferred_element_type=F32)
               + jnp.dot(tri, lf_lo, preferred_element_type=F32))
        v_b = v.astype(BF16)
        state_t = state_ref[...]

        qe = (q * jnp.exp(cum)).astype(BF16)
        out = lax.dot_general(qe, state_t.astype(BF16), _NT, preferred_element_type=F32)

        a_rows = [jnp.zeros((sub, c), F32)]
        for i in range(1, c // sub):
            anchor = cum[i * sub - 1:i * sub, :]
            kd = jnp.where(row_c < i * sub, k * jnp.exp(jnp.minimum(anchor - cum, 0.0)), 0.0)
            qd = q[i * sub:(i + 1) * sub] * jnp.exp(cum[i * sub:(i + 1) * sub] - anchor)
            a_rows.append(lax.dot_general(qd.astype(BF16), kd.astype(BF16), _NT,
                                          preferred_element_type=F32))
        a_off = jnp.concatenate(a_rows, axis=0).astype(BF16)
        out = out + jnp.dot(a_off, v_b, preferred_element_type=F32)

        diag = []
        for i in range(c // sub):
            sl = slice(i * sub, (i + 1) * sub)
            cum_i, q_i, k_i, v_i = cum[sl], q[sl], k[sl], v[sl]
            o_i = jnp.zeros((sub, LANE), F32)
            for s in range(sub):
                e = jnp.exp(jnp.minimum(cum_i - cum_i[s:s + 1, :], 0.0))
                w = jnp.where(row_s >= s, q_i * e * k_i[s:s + 1, :], 0.0)
                o_i = o_i + jnp.sum(w, axis=-1, keepdims=True) * v_i[s:s + 1, :]
            diag.append(o_i)
        out = out + jnp.concatenate(diag, axis=0)

        last = cum[c - 1:c, :]
        kdl = (k * jnp.exp(last - cum)).astype(BF16)
        state_ref[...] = jnp.exp(last) * state_t + jnp.dot(v.T.astype(BF16), kdl,
                                                            preferred_element_type=F32)

        ms = jnp.mean(out * out, axis=-1, keepdims=True)
        o_ref[pl.ds(r0, c), :] = (out * lax.rsqrt(ms + LN_EPS) * ng * _sigmoid(gr)).astype(o_ref.dtype)
        return carry

    lax.fori_loop(0, rows // c, chunk, 0)


def _hgrn_core(proj, lb, norm_g, *, batch, seq, rows=512):
    nh = N_HEADS
    t = batch * seq
    assert seq % rows == 0 and rows % HG_CHUNK == 0
    spb = seq // rows

    def head_spec(offset):
        return pl.BlockSpec((1, rows, LANE), lambda b, h, s: (offset + h, b * spb + s, 0))

    return pl.pallas_call(
        functools.partial(_hgrn_kernel, rows=rows),
        grid=(batch, nh, spb),
        in_specs=[pl.BlockSpec((1, 1, LANE), lambda b, h, s: (h, 0, 0)),
                  pl.BlockSpec((1, LANE), lambda b, h, s: (0, 0)),
                  head_spec(0), head_spec(nh), head_spec(2 * nh), head_spec(3 * nh)],
        out_specs=pl.BlockSpec((rows, LANE), lambda b, h, s: (b * spb + s, h)),
        out_shape=jax.ShapeDtypeStruct((t, nh * LANE), BF16),
        scratch_shapes=[pltpu.VMEM((LANE, LANE), F32)],
        compiler_params=_params("parallel", "parallel", "arbitrary"),
        name="hgrn2_core",
    )(lb.reshape(nh, 1, LANE), norm_g.reshape(1, LANE), proj, proj, proj, proj)


def _hgrn2_layer(h, w_in, lb_logits, norm_g, w_out, ln_g, ln_b, *, batch, seq, layer):
    lb = jnp.cumsum(jax.nn.softmax(lb_logits.astype(F32), axis=0), axis=0)[layer]
    (proj,) = _project_heads(h, w_in.astype(BF16), [4 * N_HEADS], [F32], seq=seq)
    o = _hgrn_core(proj, lb, norm_g, batch=batch, seq=seq)
    return _outproj_ln(o, w_out.astype(BF16), h, ln_g, ln_b)


def _rope_tables(seq, rot_dim):
    half = rot_dim // 2
    inv = ROPE_THETA ** (-jnp.arange(half, dtype=F32) / half)
    ang = jnp.arange(seq).astype(F32)[:, None] * inv[None, :]
    reps = LANE // rot_dim
    cos = jnp.tile(jnp.concatenate([jnp.cos(ang), jnp.cos(ang)], axis=1), (1, reps))
    sin = jnp.tile(jnp.concatenate([-jnp.sin(ang), jnp.sin(ang)], axis=1), (1, reps))
    return cos, sin


def _rope_chunk(val, cos, sin, rot_dim):
    half = rot_dim // 2
    if rot_dim == LANE:
        partner = pltpu.roll(val, half, 1)
    else:
        lane = lax.broadcasted_iota(jnp.int32, val.shape, 1)
        partner = jnp.where(lane % rot_dim < half,
                            pltpu.roll(val, LANE - half, 1), pltpu.roll(val, half, 1))
    return val * cos + partner * sin


def _softmax_init(m_ref, acc_ref):
    m_ref[...] = jnp.full_like(m_ref, NEG)
    acc_ref[...] = jnp.zeros_like(acc_ref)


def _softmax_tile(s, v, m_ref, acc_ref, rows, mask=None):
    n_lt = s.shape[1] // LANE
    part = s[:, :LANE]
    for c in range(1, n_lt):
        part = jnp.maximum(part, s[:, c * LANE:(c + 1) * LANE])
    m_prev = m_ref[rows, :]
    m_new = jnp.maximum(m_prev, jnp.max(part, axis=-1, keepdims=True))
    alpha = jnp.exp(m_prev - m_new)
    ps = []
    for c in range(n_lt):
        cols = slice(c * LANE, (c + 1) * LANE)
        pc = jnp.exp(s[:, cols] - m_new)
        if mask is not None:
            pc = jnp.where(mask[:, cols], pc, 0.0)
        ps.append(pc.astype(BF16))
    p = jnp.concatenate(ps, axis=1)
    v_aug = jnp.concatenate([v, jnp.ones_like(v)], axis=1)
    acc_ref[rows, :] = (jnp.concatenate([alpha, alpha], axis=1) * acc_ref[rows, :]
                        + jnp.dot(p, v_aug, preferred_element_type=F32))
    m_ref[rows, :] = m_new


def _softmax_result(acc_ref, rows):
    return acc_ref[rows, :LANE] / acc_ref[rows, LANE:]


def _loop_tiles(n, step, pair=True):
    if not pair:
        lax.fori_loop(0, n, lambda kj, c: (step(kj), c)[1], 0)
        return

    def body(j, carry):
        step(2 * j)
        step(2 * j + 1)
        return carry

    lax.fori_loop(0, n // 2, body, 0)

    @pl.when(n % 2 == 1)
    def _():
        step(n - 1)


def _causal_mask(qi, tq, k0, tk):
    qpos = qi * tq + lax.broadcasted_iota(jnp.int32, (tq, tk), 0)
    kpos = k0 + lax.broadcasted_iota(jnp.int32, (tq, tk), 1)
    return kpos <= qpos


def _diff_attn_kernel(lmb_ref, ng_ref, q_ref, k_ref, v_ref, o_ref, m_ref, acc_ref, *, tq, tk, hp, out_scale):
    qi = pl.program_id(2)
    lane = lax.broadcasted_iota(jnp.int32, (tq, LANE), 1)
    chains = []
    for hh in range(hp):
        q = q_ref[hh]
        zero = jnp.zeros_like(q)
        chains.append((hh, jnp.where(lane < DA_HALF, q, zero)))
        chains.append((hh, jnp.where(lane >= DA_HALF, q, zero)))
    _softmax_init(m_ref, acc_ref)

    def tile(kj, masked):
        k0 = pl.multiple_of(kj * tk, tk)
        mask = _causal_mask(qi, tq, k0, tk) if masked else None
        for c, (hh, qc) in enumerate(chains):
            s = lax.dot_general(qc, k_ref[hh, pl.ds(k0, tk), :], _NT, preferred_element_type=F32)
            if masked:
                s = jnp.where(mask, s, NEG)
            _softmax_tile(s, v_ref[hh, pl.ds(k0, tk), :], m_ref, acc_ref, slice(c * tq, (c + 1) * tq), mask)

    n_full = (qi * tq) // tk
    _loop_tiles(n_full, lambda kj: tile(kj, False))
    tile(n_full, True)

    for hh in range(hp):
        o = (_softmax_result(acc_ref, slice(2 * hh * tq, (2 * hh + 1) * tq))
             - lmb_ref[0] * _softmax_result(acc_ref, slice((2 * hh + 1) * tq, (2 * hh + 2) * tq)))
        ms = jnp.mean(o * o, axis=-1, keepdims=True)
        o_ref[:, hh * LANE:(hh + 1) * LANE] = (o * lax.rsqrt(ms + LN_EPS) * ng_ref[...] * out_scale).astype(o_ref.dtype)


def _diff_attn_core(qkv, lmb, norm_g, *, batch, seq, lam_init, tq=256, tk=512, hp=2):
    nh = N_HEADS
    t = batch * seq
    assert seq % tk == 0 and tk % tq == 0 and nh % hp == 0
    nq = seq // tq
    ng = nh // hp
    return pl.pallas_call(
        functools.partial(_diff_attn_kernel, tq=tq, tk=tk, hp=hp, out_scale=1.0 - lam_init),
        grid=(batch, ng, nq),
        in_specs=[pl.BlockSpec(memory_space=pltpu.SMEM),
                  pl.BlockSpec((1, LANE), lambda b, h, i: (0, 0)),
                  pl.BlockSpec((hp, tq, LANE), lambda b, h, i: (h, b * nq + i, 0)),
                  pl.BlockSpec((hp, seq, LANE), lambda b, h, i: (ng + h, b, 0)),
                  pl.BlockSpec((hp, seq, LANE), lambda b, h, i: (2 * ng + h, b, 0))],
        out_specs=pl.BlockSpec((tq, hp * LANE), lambda b, h, i: (b * nq + i, h)),
        out_shape=jax.ShapeDtypeStruct((t, nh * LANE), BF16),
        scratch_shapes=[pltpu.VMEM((2 * hp * tq, LANE), F32), pltpu.VMEM((2 * hp * tq, 2 * LANE), F32)],
        compiler_params=_params("parallel", "parallel", "arbitrary"),
        name="diff_attention",
    )(lmb.reshape(1), norm_g.reshape(1, LANE), qkv, qkv, qkv)


def _diff_attention_layer(h, w_in, lam, norm_g, w_out, ln_g, ln_b, w_router, *, batch, seq, layer):
    nh = N_HEADS
    lam_init = 0.8 - 0.6 * math.exp(-0.3 * layer)
    lf = lam.astype(F32)
    lmb = jnp.exp(jnp.sum(lf[0] * lf[1])) - jnp.exp(jnp.sum(lf[2] * lf[3])) + lam_init
    cos, sin = _rope_tables(seq, DA_HALF)
    q_scale = DA_HALF ** -0.5

    def epilogue(c, val, aux):
        if c < 2 * nh:
            val = _rope_chunk(val, aux[0], aux[1], DA_HALF)
        if c < nh:
            val = val * q_scale
        return val

    (qkv,) = _project_heads(h, w_in.astype(BF16), [3 * nh], [BF16], seq=seq, aux=(cos, sin), epilogue=epilogue)
    o = _diff_attn_core(qkv, lmb, norm_g, batch=batch, seq=seq, lam_init=lam_init)
    return _outproj_ln(o, w_out.astype(BF16), h, ln_g, ln_b, w_router)


def _stick_kernel(q_ref, k_ref, v_ref, o_ref, acc_ref, run_ref, *, tq, tk, sub, hp, scale):
    qi = pl.program_id(2)
    acc_ref[...] = jnp.zeros_like(acc_ref)
    run_ref[...] = jnp.zeros_like(run_ref)
    later = (lax.broadcasted_iota(jnp.int32, (sub, sub), 0)
             > lax.broadcasted_iota(jnp.int32, (sub, sub), 1)).astype(BF16)

    def tile(kj, masked):
        k0 = pl.multiple_of(kj * tk, tk)
        if masked:
            qpos = qi * tq + lax.broadcasted_iota(jnp.int32, (tq, tk), 0)
            strict = k0 + lax.broadcasted_iota(jnp.int32, (tq, tk), 1) < qpos
        for hh in range(hp):
            rows = slice(hh * tq, (hh + 1) * tq)
            z = lax.dot_general(q_ref[hh], k_ref[hh, pl.ds(k0, tk), :], _NT, preferred_element_type=F32) * scale
            neg_abs = pltpu.bitcast(pltpu.bitcast(z, jnp.uint32) | jnp.uint32(0x80000000), F32)
            soft = jnp.log(1.0 + jnp.exp(neg_abs))
            log_beta = jnp.minimum(z, 0.0) - soft
            log_keep = log_beta - z
            if masked:
                log_keep = jnp.where(strict, log_keep, 0.0)
            run = run_ref[rows, :]
            parts = [None] * (tk // sub)
            for si in reversed(range(tk // sub)):
                cols = slice(si * sub, (si + 1) * sub)
                lk = log_keep[:, cols]
                after = (jnp.dot(lk.astype(BF16), later, preferred_element_type=F32)
                         + jnp.concatenate([run] * (sub // LANE), axis=1))
                a = jnp.exp(log_beta[:, cols] + after)
                if masked:
                    a = jnp.where(strict[:, cols], a, 0.0)
                parts[si] = a.astype(BF16)
                run = run + jnp.sum(lk, axis=-1, keepdims=True)
            acc_ref[rows, :] += jnp.dot(jnp.concatenate(parts, axis=1), v_ref[hh, pl.ds(k0, tk), :],
                                        preferred_element_type=F32)
            run_ref[rows, :] = run

    n_full = (qi * tq) // tk
    tile(n_full, True)

    _loop_tiles(n_full, lambda j: tile(n_full - 1 - j, False))
    for hh in range(hp):
        o_ref[:, hh * LANE:(hh + 1) * LANE] = acc_ref[hh * tq:(hh + 1) * tq, :].astype(o_ref.dtype)


def _stick_core(qkv, *, batch, seq, tq=256, tk=512, sub=256, hp=2):
    nh = N_HEADS
    t = batch * seq
    assert seq % tk == 0 and tk % tq == 0 and tk % sub == 0 and sub % LANE == 0 and nh % hp == 0
    nq = seq // tq
    ng = nh // hp
    return pl.pallas_call(
        functools.partial(_stick_kernel, tq=tq, tk=tk, sub=sub, hp=hp, scale=HEAD_DIM ** -0.5),
        grid=(batch, ng, nq),
        in_specs=[pl.BlockSpec((hp, tq, LANE), lambda b, h, i: (h, b * nq + i, 0)),
                  pl.BlockSpec((hp, seq, LANE), lambda b, h, i: (ng + h, b, 0)),
                  pl.BlockSpec((hp, seq, LANE), lambda b, h, i: (2 * ng + h, b, 0))],
        out_specs=pl.BlockSpec((tq, hp * LANE), lambda b, h, i: (b * nq + i, h)),
        out_shape=jax.ShapeDtypeStruct((t, nh * LANE), BF16),
        scratch_shapes=[pltpu.VMEM((hp * tq, LANE), F32), pltpu.VMEM((hp * tq, LANE), F32)],
        compiler_params=_params("parallel", "parallel", "arbitrary"),
        name="stick_breaking",
    )(qkv, qkv, qkv)


def _stick_breaking_layer(h, w_in, w_out, ln_g, ln_b, w_router, *, batch, seq):
    (qkv,) = _project_heads(h, w_in.astype(BF16), [3 * N_HEADS], [BF16], seq=seq)
    o = _stick_core(qkv, batch=batch, seq=seq)
    return _outproj_ln(o, w_out.astype(BF16), h, ln_g, ln_b, w_router)


def _nsa_compress_kernel(x_ref, w1a_ref, w1b_ref, pe_ref, w2_ref, o_ref):
    x = x_ref[0]
    first = jnp.dot(x, w1a_ref[0, 0], preferred_element_type=F32)
    second = jnp.dot(x, w1b_ref[0, 0], preferred_element_type=F32)
    n16 = x.shape[0]
    second = pltpu.roll(second, n16 - 1, 0)
    pe = jnp.broadcast_to(pe_ref[0], (8, pe_ref.shape[2])).astype(BF16)
    bias = (jnp.dot(pe[:, :x.shape[1]], w1a_ref[0, 0], preferred_element_type=F32)
            + jnp.dot(pe[:, x.shape[1]:], w1b_ref[0, 0], preferred_element_type=F32))
    pre = first + second + bias[0:1, :]
    hid = jax.nn.gelu(pre)
    o_ref[0] = jnp.dot(hid.astype(BF16), w2_ref[0], preferred_element_type=F32).astype(o_ref.dtype)


def _nsa_compress(proj, pe, w1, w2, *, batch, seq, first_chunk):
    g_n = NSA_GROUPS
    n16 = seq // CMP_STRIDE
    half = CMP_STRIDE * LANE
    nch = proj.shape[0]
    x16 = proj.reshape(nch, batch * n16, half)
    w1r = w1.astype(BF16).reshape(2, 2, half, LANE)
    per = pe.astype(F32).reshape(2, 1, 2 * half)
    return pl.pallas_call(
        _nsa_compress_kernel,
        grid=(2, batch, g_n),
        in_specs=[pl.BlockSpec((1, n16, half), lambda j, b, g: (first_chunk + j * g_n + g, b, 0)),
                  pl.BlockSpec((1, 1, half, LANE), lambda j, b, g: (j, 0, 0, 0)),
                  pl.BlockSpec((1, 1, half, LANE), lambda j, b, g: (j, 1, 0, 0)),
                  pl.BlockSpec((1, 1, 2 * half), lambda j, b, g: (j, 0, 0)),
                  pl.BlockSpec((1, LANE, LANE), lambda j, b, g: (j, 0, 0))],
        out_specs=pl.BlockSpec((1, n16, LANE), lambda j, b, g: ((j * batch + b) * g_n + g, 0, 0)),
        out_shape=jax.ShapeDtypeStruct((2 * batch * g_n, n16, LANE), BF16),
        compiler_params=_params("parallel", "parallel", "parallel"),
        name="nsa_compress",
    )(x16, w1r, w1r, per, w2.astype(BF16))


def _nsa_cmp_select_kernel(q_ref, kc_ref, vc_ref, c2s_ref, o_ref, sel_ref, *, tq, scale):
    qi = pl.program_id(2)
    kc = kc_ref[0]
    vc = vc_ref[0]
    ncp = kc.shape[0]
    tpos = qi * tq + lax.broadcasted_iota(jnp.int32, (tq, 1), 0)
    cend = lax.broadcasted_iota(jnp.int32, (1, ncp), 1) * CMP_STRIDE + (CMP_BLOCK - 1)
    valid = cend <= tpos
    psum = jnp.zeros((tq, ncp), F32)
    for hh in range(NSA_HPG):
        s = lax.dot_general(q_ref[hh], kc, _NT, preferred_element_type=F32) * scale
        s = jnp.where(valid, s, NEG)
        m = jnp.max(s, axis=-1, keepdims=True)
        e = jnp.where(valid, jnp.exp(s - m), 0.0)
        l = jnp.sum(e, axis=-1, keepdims=True)
        p = e / jnp.where(l > 0.0, l, 1.0)
        o_ref[:, hh * LANE:(hh + 1) * LANE] = jnp.dot(p.astype(BF16), vc, preferred_element_type=F32)
        psum = psum + p
    imp = jnp.dot(psum.astype(BF16), c2s_ref[...], preferred_element_type=F32)
    blk = lax.broadcasted_iota(jnp.int32, imp.shape, 1)
    cur = tpos >> (SLC_BLOCK.bit_length() - 1)
    forced = (blk == 0) | (blk == cur) | (blk == cur - 1)
    imp = jnp.where(forced, FORCE, imp)
    imp = jnp.where(blk <= cur, imp, NEG)
    blk_f = blk.astype(F32)
    sel = jnp.zeros(imp.shape, jnp.bool_)
    x = imp
    for _ in range(SLC_TOPK):
        m = jnp.max(x, axis=-1, keepdims=True)
        idx = jnp.min(jnp.where(x == m, blk_f, float(LANE)), axis=-1, keepdims=True)
        hit = blk_f == idx
        sel = sel | (hit & (m > 0.5 * NEG))
        x = jnp.where(hit, -jnp.inf, x)
    sel_ref[0] = jnp.where(sel, 0.0, NEG).astype(sel_ref.dtype)


def _nsa_cmp_select(proj, kvc, c2s, *, batch, seq, tq=256):
    g_n, hpg = NSA_GROUPS, NSA_HPG
    t = batch * seq
    nq = seq // tq
    n16 = seq // CMP_STRIDE
    return pl.pallas_call(
        functools.partial(_nsa_cmp_select_kernel, tq=tq, scale=HEAD_DIM ** -0.5),
        grid=(batch, g_n, nq),
        in_specs=[pl.BlockSpec((hpg, tq, LANE), lambda b, g, i: (g, b * nq + i, 0)),
                  pl.BlockSpec((1, n16, LANE), lambda b, g, i: (b * g_n + g, 0, 0)),
                  pl.BlockSpec((1, n16, LANE), lambda b, g, i: ((batch + b) * g_n + g, 0, 0)),
                  pl.BlockSpec((n16, LANE), lambda b, g, i: (0, 0))],
        out_specs=[pl.BlockSpec((tq, hpg * LANE), lambda b, g, i: (b * nq + i, g)),
                   pl.BlockSpec((1, tq, LANE), lambda b, g, i: (g, b * nq + i, 0))],
        out_shape=[jax.ShapeDtypeStruct((t, g_n * hpg * LANE), F32),
                   jax.ShapeDtypeStruct((g_n, t, LANE), BF16)],
        compiler_params=_params("parallel", "parallel", "parallel"),
        name="nsa_compressed_select",
    )(proj, kvc, kvc, c2s)


def _nsa_selected_kernel(q_ref, sel_ref, k_ref, v_ref, et_ref, o_ref, qa_ref, m_ref, acc_ref,
                         *, tq, tk, scale):
    qi = pl.program_id(2)
    hpg = NSA_HPG
    for hh in range(hpg):
        qa_ref[hh * tq:(hh + 1) * tq, :LANE] = q_ref[hh]
        qa_ref[hh * tq:(hh + 1) * tq, LANE:] = sel_ref[0]
    _softmax_init(m_ref, acc_ref)

    def tile(kj, masked):
        k0 = pl.multiple_of(kj * tk, tk)
        ka = jnp.concatenate([k_ref[0, pl.ds(k0, tk), :], et_ref[pl.ds(k0, tk), :]], axis=1)
        v = v_ref[0, pl.ds(k0, tk), :]
        mask = _causal_mask(qi, tq, k0, tk) if masked else None
        for hh in range(hpg):
            rows = slice(hh * tq, (hh + 1) * tq)
            s = lax.dot_general(qa_ref[rows, :], ka, _NT, preferred_element_type=F32) * scale
            if masked:
                s = jnp.where(mask, s, NEG)
            _softmax_tile(s, v, m_ref, acc_ref, rows, mask)

    n_full = (qi * tq) // tk
    _loop_tiles(n_full, lambda kj: tile(kj, False))
    tile(n_full, True)
    for hh in range(hpg):
        o_ref[:, hh * LANE:(hh + 1) * LANE] = _softmax_result(acc_ref, slice(hh * tq, (hh + 1) * tq))


def _nsa_selected(proj, sel, et, *, batch, seq, q_chunk, k_chunk, v_chunk, tq=256, tk=512):
    g_n, hpg = NSA_GROUPS, NSA_HPG
    t = batch * seq
    nq = seq // tq
    assert tq & (tq - 1) == 0 and seq % tk == 0 and tk % tq == 0 and q_chunk % hpg == 0
    return pl.pallas_call(
        functools.partial(_nsa_selected_kernel, tq=tq, tk=tk, scale=HEAD_DIM ** -0.5),
        grid=(batch, g_n, nq),
        in_specs=[pl.BlockSpec((hpg, tq, LANE), lambda b, g, i: (q_chunk // hpg + g, b * nq + i, 0)),
                  pl.BlockSpec((1, tq, LANE), lambda b, g, i: (g, b * nq + i, 0)),
                  pl.BlockSpec((1, seq, LANE), lambda b, g, i: (k_chunk + g, b, 0)),
                  pl.BlockSpec((1, seq, LANE), lambda b, g, i: (v_chunk + g, b, 0)),
                  pl.BlockSpec((seq, LANE), lambda b, g, i: (0, 0))],
        out_specs=pl.BlockSpec((tq, hpg * LANE), lambda b, g, i: (b * nq + i, g)),
        out_shape=jax.ShapeDtypeStruct((t, g_n * hpg * LANE), F32),
        scratch_shapes=[pltpu.VMEM((hpg * tq, 2 * LANE), BF16), pltpu.VMEM((hpg * tq, LANE), F32),
                        pltpu.VMEM((hpg * tq, 2 * LANE), F32)],
        compiler_params=_params("parallel", "parallel", "arbitrary"),
        name="nsa_selected",
    )(proj, sel, proj, proj, et)


def _nsa_window_kernel(q_ref, k_ref, v_ref, o_ref, m_ref, acc_ref, *, tile, scale):
    qi = pl.program_id(2)
    hpg = NSA_HPG
    _softmax_init(m_ref, acc_ref)
    span = WINDOW + tile
    k0 = pl.multiple_of(jnp.maximum(qi * tile - WINDOW, 0), tile)
    k = k_ref[0, pl.ds(k0, span), :]
    v = v_ref[0, pl.ds(k0, span), :]
    qpos = qi * tile + lax.broadcasted_iota(jnp.int32, (tile, span), 0)
    kpos = k0 + lax.broadcasted_iota(jnp.int32, (tile, span), 1)
    mask = (kpos <= qpos) & (kpos > qpos - WINDOW)
    for hh in range(hpg):
        rows = slice(hh * tile, (hh + 1) * tile)
        s = lax.dot_general(q_ref[hh], k, _NT, preferred_element_type=F32) * scale
        _softmax_tile(jnp.where(mask, s, NEG), v, m_ref, acc_ref, rows, mask)
        o_ref[:, hh * LANE:(hh + 1) * LANE] = _softmax_result(acc_ref, rows)


def _nsa_window(proj, *, batch, seq, q_chunk, k_chunk, v_chunk, tile=256):
    g_n, hpg = NSA_GROUPS, NSA_HPG
    t = batch * seq
    nq = seq // tile
    assert tile & (tile - 1) == 0 and WINDOW % tile == 0 and q_chunk % hpg == 0 and seq >= WINDOW + tile
    return pl.pallas_call(
        functools.partial(_nsa_window_kernel, tile=tile, scale=HEAD_DIM ** -0.5),
        grid=(batch, g_n, nq),
        in_specs=[pl.BlockSpec((hpg, tile, LANE), lambda b, g, i: (q_chunk // hpg + g, b * nq + i, 0)),
                  pl.BlockSpec((1, seq, LANE), lambda b, g, i: (k_chunk + g, b, 0)),
                  pl.BlockSpec((1, seq, LANE), lambda b, g, i: (v_chunk + g, b, 0))],
        out_specs=pl.BlockSpec((tile, hpg * LANE), lambda b, g, i: (b * nq + i, g)),
        out_shape=jax.ShapeDtypeStruct((t, g_n * hpg * LANE), F32),
        scratch_shapes=[pltpu.VMEM((hpg * tile, LANE), F32), pltpu.VMEM((hpg * tile, 2 * LANE), F32)],
        compiler_params=_params("parallel", "parallel", "arbitrary"),
        name="nsa_window",
    )(proj, proj, proj)


def _nsa_out_kernel(oc_ref, os_ref, ow_ref, gate_ref, w_ref, h_ref, g_ref, b_ref, out_ref):
    gates = gate_ref[...]
    parts = []
    for hh in range(N_HEADS):
        cols = slice(hh * LANE, (hh + 1) * LANE)
        parts.append(gates[:, 3 * hh:3 * hh + 1] * oc_ref[:, cols]
                     + gates[:, 3 * hh + 1:3 * hh + 2] * os_ref[:, cols]
                     + gates[:, 3 * hh + 2:3 * hh + 3] * ow_ref[:, cols])
    o = jnp.concatenate(parts, axis=1).astype(BF16)
    y = jnp.dot(o, w_ref[...], preferred_element_type=F32)
    out_ref[...] = _layer_norm(DEEPNORM_ALPHA * h_ref[...] + y, g_ref[...], b_ref[...])


def _nsa_out_ln(o_cmp, o_slc, o_win, gates, w, h, g, b, *, tm=256):
    t, d = h.shape
    row = lambda i: (i, 0)
    fixed = lambda i: (0, 0)
    return pl.pallas_call(
        _nsa_out_kernel,
        grid=(t // tm,),
        in_specs=[pl.BlockSpec((tm, d), row), pl.BlockSpec((tm, d), row), pl.BlockSpec((tm, d), row),
                  pl.BlockSpec((tm, LANE), row), pl.BlockSpec((d, d), fixed), pl.BlockSpec((tm, d), row),
                  pl.BlockSpec((1, d), fixed), pl.BlockSpec((1, d), fixed)],
        out_specs=pl.BlockSpec((tm, d), row),
        out_shape=jax.ShapeDtypeStruct((t, d), F32),
        compiler_params=_params("parallel"),
        name="nsa_out_ln",
    )(o_cmp, o_slc, o_win, gates, w, h, g.reshape(1, d), b.reshape(1, d))


def _nsa_layer(h, w_in, cmp_pe, cmp_w1, cmp_w2, w_out, ln_g, ln_b, *, batch, seq):
    nh, g_n, d = N_HEADS, NSA_GROUPS, HEAD_DIM
    dm = nh * d
    n_slc = seq // SLC_BLOCK
    n16 = seq // CMP_STRIDE
    n_cmp = (seq - CMP_BLOCK) // CMP_STRIDE + 1
    assert n_slc <= LANE and n_cmp <= n16
    wq = w_in[:, :dm]
    wkv = w_in[:, dm:dm + 6 * g_n * d]
    wg = jnp.pad(w_in[:, dm + 6 * g_n * d:], ((0, 0), (0, LANE - 3 * nh)))
    w_all = jnp.concatenate([wq, wq, wkv, wg], axis=1).astype(BF16)
    kv0 = 2 * nh
    rotated = set(range(nh, 2 * nh)) | {kv0 + 2 * g_n + g for g in range(g_n)} | {kv0 + 4 * g_n + g for g in range(g_n)}
    cos, sin = _rope_tables(seq, d)

    def epilogue(c, val, aux):
        if c in rotated:
            val = _rope_chunk(val, aux[0], aux[1], d)
        if c == kv0 + 6 * g_n:
            val = _sigmoid(val)
        return val

    proj, gates = _project_heads(h, w_all, [kv0 + 6 * g_n, 1], [BF16, F32], seq=seq,
                                 aux=(cos, sin), epilogue=epilogue)
    kvc = _nsa_compress(proj, cmp_pe, cmp_w1, cmp_w2, batch=batch, seq=seq, first_chunk=kv0)
    cs = jnp.arange(n16) * CMP_STRIDE
    ss = jnp.arange(LANE) * SLC_BLOCK
    ov = jnp.clip(jnp.minimum(cs[:, None] + CMP_BLOCK, ss[None, :] + SLC_BLOCK)
                  - jnp.maximum(cs[:, None], ss[None, :]), 0, None) / CMP_BLOCK
    ov = jnp.where((jnp.arange(n16)[:, None] < n_cmp) & (jnp.arange(LANE)[None, :] < n_slc), ov, 0.0)
    c2s = ov.astype(BF16)
    et = (jnp.arange(seq)[:, None] // SLC_BLOCK == jnp.arange(LANE)[None, :]).astype(BF16)
    o_cmp, sel = _nsa_cmp_select(proj, kvc, c2s, batch=batch, seq=seq)
    o_slc = _nsa_selected(proj, sel, et, batch=batch, seq=seq, q_chunk=nh,
                          k_chunk=kv0 + 2 * g_n, v_chunk=kv0 + 3 * g_n)
    o_win = _nsa_window(proj, batch=batch, seq=seq, q_chunk=nh,
                        k_chunk=kv0 + 4 * g_n, v_chunk=kv0 + 5 * g_n)
    return _nsa_out_ln(o_cmp, o_slc, o_win, gates[0], w_out.astype(BF16), h, ln_g, ln_b)


def _moe_expert_kernel(te_ref, nu_ref, x_ref, wg_ref, wu_ref, wd_ref, out_ref, acc_ref):
    i = pl.program_id(0)
    k = pl.program_id(1)
    last = pl.num_programs(1) - 1
    used = i < nu_ref[0]

    @pl.when(used)
    def _():
        @pl.when(k == 0)
        def _():
            acc_ref[...] = jnp.zeros_like(acc_ref)

        x = x_ref[...]
        gate = jnp.dot(x, wg_ref[0], preferred_element_type=F32)
        up = jnp.dot(x, wu_ref[0], preferred_element_type=F32)
        act = (gate * _sigmoid(gate) * up).astype(BF16)
        acc_ref[...] += jnp.dot(act, wd_ref[0], preferred_element_type=F32)

        @pl.when(k == last)
        def _():
            out_ref[...] = acc_ref[...]

    @pl.when(jnp.logical_not(used) & (k == last))
    def _():
        out_ref[...] = jnp.zeros_like(out_ref)


def _moe_experts(xs, tile_expert, n_used, wg, wu, wd, *, tm, tf=512):
    n_slots, d = xs.shape
    ffe = wg.shape[2]
    assert n_slots % tm == 0 and ffe % tf == 0
    grid_spec = pltpu.PrefetchScalarGridSpec(
        num_scalar_prefetch=2,
        grid=(n_slots // tm, ffe // tf),
        in_specs=[pl.BlockSpec((tm, d), lambda i, k, te, nu: (i, 0)),
                  pl.BlockSpec((1, d, tf), lambda i, k, te, nu: (te[i], 0, k)),
                  pl.BlockSpec((1, d, tf), lambda i, k, te, nu: (te[i], 0, k)),
                  pl.BlockSpec((1, tf, d), lambda i, k, te, nu: (te[i], k, 0))],
        out_specs=pl.BlockSpec((tm, d), lambda i, k, te, nu: (i, 0)),
        scratch_shapes=[pltpu.VMEM((tm, d), F32)])
    return pl.pallas_call(
        _moe_expert_kernel,
        grid_spec=grid_spec,
        out_shape=jax.ShapeDtypeStruct((n_slots, d), F32),
        compiler_params=_params("parallel", "arbitrary"),
        name="moe_experts",
    )(tile_expert, n_used, xs, wg, wu, wd)


def _moe_combine_kernel(y_ref, gate_ref, h_ref, g_ref, b_ref, out_ref):
    d = h_ref.shape[1]
    gates = gate_ref[...]
    y = gates[:, 0:1] * y_ref[:, :d] + gates[:, 1:2] * y_ref[:, d:]
    out_ref[...] = _layer_norm(DEEPNORM_ALPHA * h_ref[...] + y, g_ref[...], b_ref[...])


def _moe_combine_ln(y_pair, gates, h, g, b, *, tm=512):
    t, d = h.shape
    row = lambda i: (i, 0)
    fixed = lambda i: (0, 0)
    return pl.pallas_call(
        _moe_combine_kernel,
        grid=(t // tm,),
        in_specs=[pl.BlockSpec((tm, TOP_K * d), row), pl.BlockSpec((tm, TOP_K), row), pl.BlockSpec((tm, d), row),
                  pl.BlockSpec((1, d), fixed), pl.BlockSpec((1, d), fixed)],
        out_specs=pl.BlockSpec((tm, d), row),
        out_shape=jax.ShapeDtypeStruct((t, d), F32),
        compiler_params=_params("parallel"),
        name="moe_combine_ln",
    )(y_pair, gates, h, g.reshape(1, d), b.reshape(1, d))


def _moe_layer(h, h_b, logits, wg, wu, wd, ln_g, ln_b, *, tm=512):
    t, d = h.shape
    ne = N_EXPERTS
    top_val, top_idx = lax.top_k(logits[:, :ne], TOP_K)
    gates = jax.nn.softmax(top_val, axis=-1)
    n_assign = t * TOP_K
    flat_e = top_idx.reshape(n_assign)
    flat_tok = jnp.repeat(jnp.arange(t, dtype=jnp.int32), TOP_K)
    onehot = (flat_e[:, None] == jnp.arange(ne, dtype=flat_e.dtype)[None, :]).astype(jnp.int32)
    before = jnp.cumsum(onehot, axis=0) - onehot
    rank = jnp.sum(before * onehot, axis=1)
    counts = jnp.sum(onehot, axis=0)
    padded = (counts + tm - 1) // tm * tm
    pad_end = jnp.cumsum(padded)
    pad_start = pad_end - padded
    dest = (pad_start[flat_e] + rank).astype(jnp.int32)
    n_tiles = -(-n_assign // tm) + ne
    slot_tok = jnp.zeros((n_tiles * tm,), jnp.int32).at[dest].set(flat_tok)
    tile_expert = jnp.minimum(jnp.searchsorted(pad_end, jnp.arange(n_tiles) * tm, side='right'),
                              ne - 1).astype(jnp.int32)
    n_used = (pad_end[-1] // tm).astype(jnp.int32).reshape(1)
    xs = h_b[slot_tok]
    ys = _moe_experts(xs, tile_expert, n_used, wg, wu, wd, tm=tm)
    y_pair = ys[dest].reshape(t, TOP_K * d)
    return _moe_combine_ln(y_pair, gates, h, ln_g, ln_b)


def kernel(x, hg_w_in, hg_lb, hg_norm_g, hg_w_out, da_w_in, da_lam, da_norm_g, da_w_out, nsa_w_in, nsa_cmp_pe, nsa_cmp_w1, nsa_cmp_w2, nsa_w_out, sb_w_in, sb_w_out, ffn_w_gate, ffn_w_up, ffn_w_down, moe_w_router, moe_w_gate, moe_w_up, moe_w_down, ln_g, ln_b):
    batch, seq, d = x.shape
    h = x.reshape(batch * seq, d)

    def router(j):
        return jnp.pad(moe_w_router[j], ((0, 0), (0, LANE - N_EXPERTS))).astype(BF16)

    def dense_ffn(h, j, layer):
        return _ffn_ln(h, ffn_w_gate[j].astype(BF16), ffn_w_up[j].astype(BF16), ffn_w_down[j].astype(BF16),
                       ln_g[layer, 1], ln_b[layer, 1])

    def expert_ffn(h, h_b, logits, j, layer):
        return _moe_layer(h, h_b, logits, moe_w_gate[j].astype(BF16), moe_w_up[j].astype(BF16),
                          moe_w_down[j].astype(BF16), ln_g[layer, 1], ln_b[layer, 1])

    h = _hgrn2_layer(h, hg_w_in, hg_lb, hg_norm_g, hg_w_out, ln_g[0, 0], ln_b[0, 0],
                     batch=batch, seq=seq, layer=0)
    h = dense_ffn(h, 0, 0)
    h, h_b, logits = _diff_attention_layer(h, da_w_in, da_lam, da_norm_g, da_w_out, ln_g[1, 0], ln_b[1, 0],
                                           router(0), batch=batch, seq=seq, layer=1)
    h = expert_ffn(h, h_b, logits, 0, 1)
    h = _nsa_layer(h, nsa_w_in, nsa_cmp_pe, nsa_cmp_w1, nsa_cmp_w2, nsa_w_out, ln_g[2, 0], ln_b[2, 0],
                   batch=batch, seq=seq)
    h = dense_ffn(h, 1, 2)
    h, h_b, logits = _stick_breaking_layer(h, sb_w_in, sb_w_out, ln_g[3, 0], ln_b[3, 0], router(1),
                                           batch=batch, seq=seq)
    h = expert_ffn(h, h_b, logits, 1, 3)
    return h.reshape(batch, seq, d)
```

```python
import functools
import math

import jax
import jax.numpy as jnp
from jax import lax
from jax.experimental import pallas as pl
from jax.experimental.pallas import tpu as pltpu

F32 = jnp.float32
BF16 = jnp.bfloat16

LANE = 128
VMEM_LIMIT_BYTES = 56 * 1024 * 1024

N_HEADS = 8
HEAD_DIM = 128
ROPE_THETA = 10000.0
HG_CHUNK = 64
HG_SUB = 16
DA_HALF = 64
NSA_GROUPS = 2
NSA_HPG = 4
CMP_BLOCK = 32
CMP_STRIDE = 16
SLC_BLOCK = 64
SLC_TOPK = 16
WINDOW = 512
N_EXPERTS = 8
TOP_K = 2
LN_EPS = 1e-5
NEG = -1e30
FORCE = 1e9
DEPTH = 4
DEEPNORM_ALPHA = (2 * DEPTH) ** 0.25

_NT = (((1,), (1,)), ((), ()))
_EXP_F32_ZERO_BELOW = -104.0

def _params(*semantics):
    return pltpu.CompilerParams(dimension_semantics=semantics, vmem_limit_bytes=VMEM_LIMIT_BYTES)


def _sigmoid(x):
    return 1.0 / (1.0 + jnp.exp(-x))


def _layer_norm(z, g, b):
    mu = jnp.mean(z, axis=-1, keepdims=True)
    zc = z - mu
    var = jnp.mean(zc * zc, axis=-1, keepdims=True)
    return zc * lax.rsqrt(var + LN_EPS) * g + b


def _proj_kernel(*refs, n_aux, groups, epilogue):
    x_ref, w_ref = refs[0], refs[1]
    aux_refs = refs[2:2 + n_aux]
    out_refs = refs[2 + n_aux:]
    x = x_ref[...].astype(BF16)
    aux = [r[...] for r in aux_refs]
    c = 0
    for out_ref, n in zip(out_refs, groups):
        for local in range(n):
            val = jnp.dot(x, w_ref[:, c * LANE:(c + 1) * LANE], preferred_element_type=F32)
            if epilogue is not None:
                val = epilogue(c, val, aux)
            out_ref[local] = val.astype(out_ref.dtype)
            c += 1


def _project_heads(x, w, groups, dtypes, *, seq, aux=(), epilogue=None, tm=512):
    t, d = x.shape
    n_chunks = sum(groups)
    assert w.shape == (d, n_chunks * LANE) and t % tm == 0 and seq % tm == 0
    tiles_per_seq = seq // tm
    in_specs = [pl.BlockSpec((tm, d), lambda i: (i, 0)),
                pl.BlockSpec((d, n_chunks * LANE), lambda i: (0, 0))]
    for a in aux:
        assert a.shape == (seq, LANE)
        in_specs.append(pl.BlockSpec((tm, LANE), lambda i: (i % tiles_per_seq, 0)))
    out_shape = [jax.ShapeDtypeStruct((n, t, LANE), dt) for n, dt in zip(groups, dtypes)]
    out_specs = [pl.BlockSpec((n, tm, LANE), lambda i: (0, i, 0)) for n in groups]
    return pl.pallas_call(
        functools.partial(_proj_kernel, n_aux=len(aux), groups=tuple(groups), epilogue=epilogue),
        grid=(t // tm,),
        in_specs=in_specs,
        out_specs=out_specs,
        out_shape=out_shape,
        compiler_params=_params("parallel"),
        name="project_heads",
    )(x, w, *aux)


def _outproj_ln_kernel(*refs, with_router):
    if with_router:
        o_ref, w_ref, h_ref, g_ref, b_ref, wr_ref, out_ref, outb_ref, logit_ref = refs
    else:
        o_ref, w_ref, h_ref, g_ref, b_ref, out_ref = refs
    y = jnp.dot(o_ref[...], w_ref[...], preferred_element_type=F32)
    hn = _layer_norm(DEEPNORM_ALPHA * h_ref[...] + y, g_ref[...], b_ref[...])
    out_ref[...] = hn
    if with_router:
        hb = hn.astype(BF16)
        outb_ref[...] = hb
        logit_ref[...] = jnp.dot(hb, wr_ref[...], preferred_element_type=F32)


def _outproj_ln(o, w, h, g, b, w_router=None, *, tm=512):
    t, d = h.shape
    assert o.shape == (t, d) and t % tm == 0
    with_router = w_router is not None
    row = lambda i: (i, 0)
    fixed = lambda i: (0, 0)
    in_specs = [pl.BlockSpec((tm, d), row), pl.BlockSpec((d, d), fixed), pl.BlockSpec((tm, d), row),
                pl.BlockSpec((1, d), fixed), pl.BlockSpec((1, d), fixed)]
    args = [o, w, h, g.reshape(1, d), b.reshape(1, d)]
    out_shape = [jax.ShapeDtypeStruct((t, d), F32)]
    out_specs = [pl.BlockSpec((tm, d), row)]
    if with_router:
        in_specs.append(pl.BlockSpec((d, LANE), fixed))
        args.append(w_router)
        out_shape += [jax.ShapeDtypeStruct((t, d), BF16), jax.ShapeDtypeStruct((t, LANE), F32)]
        out_specs += [pl.BlockSpec((tm, d), row), pl.BlockSpec((tm, LANE), row)]
    res = pl.pallas_call(
        functools.partial(_outproj_ln_kernel, with_router=with_router),
        grid=(t // tm,),
        in_specs=in_specs,
        out_specs=out_specs,
        out_shape=out_shape,
        compiler_params=_params("parallel"),
        name="outproj_ln",
    )(*args)
    return res if with_router else res[0]


def _ffn_kernel(h_ref, wg_ref, wu_ref, wd_ref, g_ref, b_ref, out_ref, xb_ref, acc_ref):
    k = pl.program_id(1)

    @pl.when(k == 0)
    def _():
        xb_ref[...] = h_ref[...].astype(BF16)
        acc_ref[...] = jnp.zeros_like(acc_ref)

    x = xb_ref[...]
    gate = jnp.dot(x, wg_ref[...], preferred_element_type=F32)
    up = jnp.dot(x, wu_ref[...], preferred_element_type=F32)
    act = (gate * _sigmoid(gate) * up).astype(BF16)
    acc_ref[...] += jnp.dot(act, wd_ref[...], preferred_element_type=F32)

    @pl.when(k == pl.num_programs(1) - 1)
    def _():
        out_ref[...] = _layer_norm(DEEPNORM_ALPHA * h_ref[...] + acc_ref[...], g_ref[...], b_ref[...])


def _ffn_ln(h, wg, wu, wd, g, b, *, tm=512, tf=256):
    t, d = h.shape
    ff = wg.shape[1]
    assert t % tm == 0 and ff % tf == 0
    return pl.pallas_call(
        _ffn_kernel,
        grid=(t // tm, ff // tf),
        in_specs=[pl.BlockSpec((tm, d), lambda i, k: (i, 0)),
                  pl.BlockSpec((d, tf), lambda i, k: (0, k)),
                  pl.BlockSpec((d, tf), lambda i, k: (0, k)),
                  pl.BlockSpec((tf, d), lambda i, k: (k, 0)),
                  pl.BlockSpec((1, d), lambda i, k: (0, 0)),
                  pl.BlockSpec((1, d), lambda i, k: (0, 0))],
        out_specs=pl.BlockSpec((tm, d), lambda i, k: (i, 0)),
        out_shape=jax.ShapeDtypeStruct((t, d), F32),
        scratch_shapes=[pltpu.VMEM((tm, d), BF16), pltpu.VMEM((tm, d), F32)],
        compiler_params=_params("parallel", "arbitrary"),
        name="swiglu_ln",
    )(h, wg, wu, wd, g.reshape(1, d), b.reshape(1, d))


def _hgrn_kernel(lb_ref, ng_ref, q_ref, f_ref, i_ref, g_ref, o_ref, state_ref, *, rows, hp):
    c, sub = HG_CHUNK, HG_SUB

    @pl.when(pl.program_id(2) == 0)
    def _():
        state_ref[...] = jnp.zeros_like(state_ref)

    ng = ng_ref[...]
    r_io = lax.broadcasted_iota(jnp.int32, (c, c), 0)
    c_io = lax.broadcasted_iota(jnp.int32, (c, c), 1)
    tri = (r_io >= c_io).astype(BF16)
    row_c = lax.broadcasted_iota(jnp.int32, (c, LANE), 0)
    row_s = lax.broadcasted_iota(jnp.int32, (sub, LANE), 0)

    def head_chunk(hh, r0):
        lb = lb_ref[hh]
        qr = q_ref[hh, pl.ds(r0, c), :]
        fr = f_ref[hh, pl.ds(r0, c), :]
        v = i_ref[hh, pl.ds(r0, c), :]
        gr = g_ref[hh, pl.ds(r0, c), :]
        forget = lb + (1.0 - lb) * _sigmoid(fr)
        lf = jnp.log(forget)
        k = 1.0 - forget
        q = qr * _sigmoid(qr)
        lf_hi = lf.astype(BF16)
        lf_lo = (lf - lf_hi.astype(F32)).astype(BF16)
        cum = (jnp.dot(tri, lf_hi, preferred_element_type=F32)
               + jnp.dot(tri, lf_lo, preferred_element_type=F32))
        v_b = v.astype(BF16)
        state_t = state_ref[hh]

        qe = (q * jnp.exp(cum)).astype(BF16)
        out = lax.dot_general(qe, state_t.astype(BF16), _NT, preferred_element_type=F32)

        a_rows = [jnp.zeros((sub, c), F32)]
        for i in range(1, c // sub):
            anchor = cum[i * sub - 1:i * sub, :]
            kd = jnp.where(row_c < i * sub, k * jnp.exp(jnp.minimum(anchor - cum, 0.0)), 0.0)
            qd = q[i * sub:(i + 1) * sub] * jnp.exp(cum[i * sub:(i + 1) * sub] - anchor)
            a_rows.append(lax.dot_general(qd.astype(BF16), kd.astype(BF16), _NT,
                                          preferred_element_type=F32))
        a_off = jnp.concatenate(a_rows, axis=0).astype(BF16)
        out = out + jnp.dot(a_off, v_b, preferred_element_type=F32)

        diag = []
        for i in range(c // sub):
            sl = slice(i * sub, (i + 1) * sub)
            cum_i, q_i, k_i, v_i = cum[sl], q[sl], k[sl], v[sl]
            o_i = jnp.zeros((sub, LANE), F32)
            for s in range(sub):
                e = jnp.exp(jnp.minimum(cum_i - cum_i[s:s + 1, :], 0.0))
                w = jnp.where(row_s >= s, q_i * e * k_i[s:s + 1, :], 0.0)
                o_i = o_i + jnp.sum(w, axis=-1, keepdims=True) * v_i[s:s + 1, :]
            diag.append(o_i)
        out = out + jnp.concatenate(diag, axis=0)

        last = cum[c - 1:c, :]
        kdl = (k * jnp.exp(last - cum)).astype(BF16)
        state_ref[hh] = jnp.exp(last) * state_t + jnp.dot(v.T.astype(BF16), kdl,
                                                           preferred_element_type=F32)

        ms = jnp.mean(out * out, axis=-1, keepdims=True)
        o_ref[pl.ds(r0, c), hh * LANE:(hh + 1) * LANE] = (
            out * lax.rsqrt(ms + LN_EPS) * ng * _sigmoid(gr)).astype(o_ref.dtype)

    def chunk(ci, carry):
        r0 = pl.multiple_of(ci * c, c)
        for hh in range(hp):
            head_chunk(hh, r0)
        return carry

    lax.fori_loop(0, rows // c, chunk, 0)


def _hgrn_core(proj, lb, norm_g, *, batch, seq, rows=512, hp=4):
    nh = N_HEADS
    t = batch * seq
    assert seq % rows == 0 and rows % HG_CHUNK == 0 and nh % hp == 0
    spb = seq // rows
    ng = nh // hp

    def head_spec(offset):
        return pl.BlockSpec((hp, rows, LANE), lambda b, h, s: (offset + h, b * spb + s, 0))

    return pl.pallas_call(
        functools.partial(_hgrn_kernel, rows=rows, hp=hp),
        grid=(batch, ng, spb),
        in_specs=[pl.BlockSpec((hp, 1, LANE), lambda b, h, s: (h, 0, 0)),
                  pl.BlockSpec((1, LANE), lambda b, h, s: (0, 0)),
                  head_spec(0), head_spec(ng), head_spec(2 * ng), head_spec(3 * ng)],
        out_specs=pl.BlockSpec((rows, hp * LANE), lambda b, h, s: (b * spb + s, h)),
        out_shape=jax.ShapeDtypeStruct((t, nh * LANE), BF16),
        scratch_shapes=[pltpu.VMEM((hp, LANE, LANE), F32)],
        compiler_params=_params("parallel", "parallel", "arbitrary"),
        name="hgrn2_core",
    )(lb.reshape(nh, 1, LANE), norm_g.reshape(1, LANE), proj, proj, proj, proj)


def _hgrn2_layer(h, w_in, lb_logits, norm_g, w_out, ln_g, ln_b, *, batch, seq, layer):
    lb = jnp.cumsum(jax.nn.softmax(lb_logits.astype(F32), axis=0), axis=0)[layer]
    (proj,) = _project_heads(h, w_in.astype(BF16), [4 * N_HEADS], [F32], seq=seq)
    o = _hgrn_core(proj, lb, norm_g, batch=batch, seq=seq)
    return _outproj_ln(o, w_out.astype(BF16), h, ln_g, ln_b)


def _rope_tables(seq, rot_dim):
    half = rot_dim // 2
    inv = ROPE_THETA ** (-jnp.arange(half, dtype=F32) / half)
    ang = jnp.arange(seq).astype(F32)[:, None] * inv[None, :]
    reps = LANE // rot_dim
    cos = jnp.tile(jnp.concatenate([jnp.cos(ang), jnp.cos(ang)], axis=1), (1, reps))
    sin = jnp.tile(jnp.concatenate([-jnp.sin(ang), jnp.sin(ang)], axis=1), (1, reps))
    return cos, sin


def _rope_chunk(val, cos, sin, rot_dim):
    half = rot_dim // 2
    if rot_dim == LANE:
        partner = pltpu.roll(val, half, 1)
    else:
        lane = lax.broadcasted_iota(jnp.int32, val.shape, 1)
        partner = jnp.where(lane % rot_dim < half,
                            pltpu.roll(val, LANE - half, 1), pltpu.roll(val, half, 1))
    return val * cos + partner * sin


def _softmax_init(m_ref, acc_ref):
    m_ref[...] = jnp.full_like(m_ref, NEG)
    acc_ref[...] = jnp.zeros_like(acc_ref)


def _softmax_tile(s, v, m_ref, acc_ref, rows, mask=None):
    n_lt = s.shape[1] // LANE
    part = s[:, :LANE]
    for c in range(1, n_lt):
        part = jnp.maximum(part, s[:, c * LANE:(c + 1) * LANE])
    m_prev = m_ref[rows, :]
    m_new = jnp.maximum(m_prev, jnp.max(part, axis=-1, keepdims=True))
    alpha = jnp.exp(m_prev - m_new)
    ps = []
    for c in range(n_lt):
        cols = slice(c * LANE, (c + 1) * LANE)
        pc = jnp.exp(s[:, cols] - m_new)
        if mask is not None:
            pc = jnp.where(mask[:, cols], pc, 0.0)
        ps.append(pc.astype(BF16))
    p = jnp.concatenate(ps, axis=1)
    v_aug = jnp.concatenate([v, jnp.ones_like(v)], axis=1)
    acc_ref[rows, :] = (jnp.concatenate([alpha, alpha], axis=1) * acc_ref[rows, :]
                        + jnp.dot(p, v_aug, preferred_element_type=F32))
    m_ref[rows, :] = m_new


def _softmax_result(acc_ref, rows):
    return acc_ref[rows, :LANE] / acc_ref[rows, LANE:]


def _loop_tiles(n, step, group=2):
    def body(j, carry):
        for u in range(group):
            step(group * j + u)
        return carry

    lax.fori_loop(0, n // group, body, 0)
    done = n // group * group
    if group == 4:
        @pl.when(n - done >= 2)
        def _():
            step(done)
            step(done + 1)

    @pl.when(n % 2 == 1)
    def _():
        step(n - 1)


def _loop_tiles_while(n, step, alive):
    def body(carry):
        step(carry[0])
        return carry[0] + 1, alive()

    lax.while_loop(lambda c: (c[0] < n) & c[1], body, (jnp.int32(0), alive()))


def _causal_mask(qi, tq, k0, tk):
    qpos = qi * tq + lax.broadcasted_iota(jnp.int32, (tq, tk), 0)
    kpos = k0 + lax.broadcasted_iota(jnp.int32, (tq, tk), 1)
    return kpos <= qpos


def _diff_attn_kernel(lmb_ref, ng_ref, q_ref, k_ref, v_ref, o_ref, m_ref, acc_ref, *, tq, tk, hp, out_scale):
    qi = pl.program_id(2)
    lane = lax.broadcasted_iota(jnp.int32, (tq, LANE), 1)
    chains = []
    for hh in range(hp):
        q = q_ref[hh]
        zero = jnp.zeros_like(q)
        chains.append((hh, jnp.where(lane < DA_HALF, q, zero)))
        chains.append((hh, jnp.where(lane >= DA_HALF, q, zero)))
    _softmax_init(m_ref, acc_ref)

    def tile(kj, masked):
        k0 = pl.multiple_of(kj * tk, tk)
        mask = _causal_mask(qi, tq, k0, tk) if masked else None
        for c, (hh, qc) in enumerate(chains):
            s = lax.dot_general(qc, k_ref[hh, pl.ds(k0, tk), :], _NT, preferred_element_type=F32)
            if masked:
                s = jnp.where(mask, s, NEG)
            _softmax_tile(s, v_ref[hh, pl.ds(k0, tk), :], m_ref, acc_ref, slice(c * tq, (c + 1) * tq), mask)

    n_full = (qi * tq) // tk
    _loop_tiles(n_full, lambda kj: tile(kj, False), group=4)
    tile(n_full, True)

    for hh in range(hp):
        o = (_softmax_result(acc_ref, slice(2 * hh * tq, (2 * hh + 1) * tq))
             - lmb_ref[0] * _softmax_result(acc_ref, slice((2 * hh + 1) * tq, (2 * hh + 2) * tq)))
        ms = jnp.mean(o * o, axis=-1, keepdims=True)
        o_ref[:, hh * LANE:(hh + 1) * LANE] = (o * lax.rsqrt(ms + LN_EPS) * ng_ref[...] * out_scale).astype(o_ref.dtype)


def _diff_attn_core(qkv, lmb, norm_g, *, batch, seq, lam_init, tq=256, tk=512, hp=2):
    nh = N_HEADS
    t = batch * seq
    assert seq % tk == 0 and tk % tq == 0 and nh % hp == 0
    nq = seq // tq
    ng = nh // hp
    return pl.pallas_call(
        functools.partial(_diff_attn_kernel, tq=tq, tk=tk, hp=hp, out_scale=1.0 - lam_init),
        grid=(batch, ng, nq),
        in_specs=[pl.BlockSpec(memory_space=pltpu.SMEM),
                  pl.BlockSpec((1, LANE), lambda b, h, i: (0, 0)),
                  pl.BlockSpec((hp, tq, LANE), lambda b, h, i: (h, b * nq + i, 0)),
                  pl.BlockSpec((hp, seq, LANE), lambda b, h, i: (ng + h, b, 0)),
                  pl.BlockSpec((hp, seq, LANE), lambda b, h, i: (2 * ng + h, b, 0))],
        out_specs=pl.BlockSpec((tq, hp * LANE), lambda b, h, i: (b * nq + i, h)),
        out_shape=jax.ShapeDtypeStruct((t, nh * LANE), BF16),
        scratch_shapes=[pltpu.VMEM((2 * hp * tq, LANE), F32), pltpu.VMEM((2 * hp * tq, 2 * LANE), F32)],
        compiler_params=_params("parallel", "parallel", "arbitrary"),
        name="diff_attention",
    )(lmb.reshape(1), norm_g.reshape(1, LANE), qkv, qkv, qkv)


def _diff_attention_layer(h, w_in, lam, norm_g, w_out, ln_g, ln_b, w_router, *, batch, seq, layer):
    nh = N_HEADS
    lam_init = 0.8 - 0.6 * math.exp(-0.3 * layer)
    lf = lam.astype(F32)
    lmb = jnp.exp(jnp.sum(lf[0] * lf[1])) - jnp.exp(jnp.sum(lf[2] * lf[3])) + lam_init
    cos, sin = _rope_tables(seq, DA_HALF)
    q_scale = DA_HALF ** -0.5

    def epilogue(c, val, aux):
        if c < 2 * nh:
            val = _rope_chunk(val, aux[0], aux[1], DA_HALF)
        if c < nh:
            val = val * q_scale
        return val

    (qkv,) = _project_heads(h, w_in.astype(BF16), [3 * nh], [BF16], seq=seq, aux=(cos, sin), epilogue=epilogue)
    o = _diff_attn_core(qkv, lmb, norm_g, batch=batch, seq=seq, lam_init=lam_init)
    return _outproj_ln(o, w_out.astype(BF16), h, ln_g, ln_b, w_router)


def _stick_kernel(q_ref, k_ref, v_ref, o_ref, acc_ref, run_ref, *, tq, tk, sub, hp, scale):
    qi = pl.program_id(2)
    acc_ref[...] = jnp.zeros_like(acc_ref)
    run_ref[...] = jnp.zeros_like(run_ref)
    later = (lax.broadcasted_iota(jnp.int32, (sub, sub), 0)
             > lax.broadcasted_iota(jnp.int32, (sub, sub), 1)).astype(BF16)

    def tile(kj, masked):
        k0 = pl.multiple_of(kj * tk, tk)
        if masked:
            qpos = qi * tq + lax.broadcasted_iota(jnp.int32, (tq, tk), 0)
            strict = k0 + lax.broadcasted_iota(jnp.int32, (tq, tk), 1) < qpos
        for hh in range(hp):
            rows = slice(hh * tq, (hh + 1) * tq)
            z = lax.dot_general(q_ref[hh], k_ref[hh, pl.ds(k0, tk), :], _NT, preferred_element_type=F32) * scale
            neg_abs = pltpu.bitcast(pltpu.bitcast(z, jnp.uint32) | jnp.uint32(0x80000000), F32)
            soft = jnp.log(1.0 + jnp.exp(neg_abs))
            log_beta = jnp.minimum(z, 0.0) - soft
            log_keep = log_beta - z
            if masked:
                log_keep = jnp.where(strict, log_keep, 0.0)
            run = run_ref[rows, :]
            parts = [None] * (tk // sub)
            for si in reversed(range(tk // sub)):
                cols = slice(si * sub, (si + 1) * sub)
                lk = log_keep[:, cols]
                after = (jnp.dot(lk.astype(BF16), later, preferred_element_type=F32)
                         + jnp.concatenate([run] * (sub // LANE), axis=1))
                a = jnp.exp(log_beta[:, cols] + after)
                if masked:
                    a = jnp.where(strict[:, cols], a, 0.0)
                parts[si] = a.astype(BF16)
                run = run + jnp.sum(lk, axis=-1, keepdims=True)
            acc_ref[rows, :] += jnp.dot(jnp.concatenate(parts, axis=1), v_ref[hh, pl.ds(k0, tk), :],
                                        preferred_element_type=F32)
            run_ref[rows, :] = run

    n_full = (qi * tq) // tk
    tile(n_full, True)

    _loop_tiles_while(n_full, lambda j: tile(n_full - 1 - j, False),
                      alive=lambda: jnp.max(run_ref[...]) > _EXP_F32_ZERO_BELOW)
    for hh in range(hp):
        o_ref[:, hh * LANE:(hh + 1) * LANE] = acc_ref[hh * tq:(hh + 1) * tq, :].astype(o_ref.dtype)


def _stick_core(qkv, *, batch, seq, tq=256, tk=512, sub=256, hp=4):
    nh = N_HEADS
    t = batch * seq
    assert seq % tk == 0 and tk % tq == 0 and tk % sub == 0 and sub % LANE == 0 and nh % hp == 0
    nq = seq // tq
    ng = nh // hp
    return pl.pallas_call(
        functools.partial(_stick_kernel, tq=tq, tk=tk, sub=sub, hp=hp, scale=HEAD_DIM ** -0.5),
        grid=(batch, ng, nq),
        in_specs=[pl.BlockSpec((hp, tq, LANE), lambda b, h, i: (h, b * nq + i, 0)),
                  pl.BlockSpec((hp, seq, LANE), lambda b, h, i: (ng + h, b, 0)),
                  pl.BlockSpec((hp, seq, LANE), lambda b, h, i: (2 * ng + h, b, 0))],
        out_specs=pl.BlockSpec((tq, hp * LANE), lambda b, h, i: (b * nq + i, h)),
        out_shape=jax.ShapeDtypeStruct((t, nh * LANE), BF16),
        scratch_shapes=[pltpu.VMEM((hp * tq, LANE), F32), pltpu.VMEM((hp * tq, LANE), F32)],
        compiler_params=_params("parallel", "parallel", "arbitrary"),
        name="stick_breaking",
    )(qkv, qkv, qkv)


def _stick_breaking_layer(h, w_in, w_out, ln_g, ln_b, w_router, *, batch, seq):
    (qkv,) = _project_heads(h, w_in.astype(BF16), [3 * N_HEADS], [BF16], seq=seq)
    o = _stick_core(qkv, batch=batch, seq=seq)
    return _outproj_ln(o, w_out.astype(BF16), h, ln_g, ln_b, w_router)


def _nsa_compress_kernel(x_ref, w1a_ref, w1b_ref, pe_ref, w2_ref, o_ref):
    x = x_ref[0]
    first = jnp.dot(x, w1a_ref[0, 0], preferred_element_type=F32)
    second = jnp.dot(x, w1b_ref[0, 0], preferred_element_type=F32)
    n16 = x.shape[0]
    second = pltpu.roll(second, n16 - 1, 0)
    pe = jnp.broadcast_to(pe_ref[0], (8, pe_ref.shape[2])).astype(BF16)
    bias = (jnp.dot(pe[:, :x.shape[1]], w1a_ref[0, 0], preferred_element_type=F32)
            + jnp.dot(pe[:, x.shape[1]:], w1b_ref[0, 0], preferred_element_type=F32))
    pre = first + second + bias[0:1, :]
    hid = jax.nn.gelu(pre)
    o_ref[0] = jnp.dot(hid.astype(BF16), w2_ref[0], preferred_element_type=F32).astype(o_ref.dtype)


def _nsa_compress(proj, pe, w1, w2, *, batch, seq, first_chunk):
    g_n = NSA_GROUPS
    n16 = seq // CMP_STRIDE
    half = CMP_STRIDE * LANE
    nch = proj.shape[0]
    x16 = proj.reshape(nch, batch * n16, half)
    w1r = w1.astype(BF16).reshape(2, 2, half, LANE)
    per = pe.astype(F32).reshape(2, 1, 2 * half)
    return pl.pallas_call(
        _nsa_compress_kernel,
        grid=(2, batch, g_n),
        in_specs=[pl.BlockSpec((1, n16, half), lambda j, b, g: (first_chunk + j * g_n + g, b, 0)),
                  pl.BlockSpec((1, 1, half, LANE), lambda j, b, g: (j, 0, 0, 0)),
                  pl.BlockSpec((1, 1, half, LANE), lambda j, b, g: (j, 1, 0, 0)),
                  pl.BlockSpec((1, 1, 2 * half), lambda j, b, g: (j, 0, 0)),
                  pl.BlockSpec((1, LANE, LANE), lambda j, b, g: (j, 0, 0))],
        out_specs=pl.BlockSpec((1, n16, LANE), lambda j, b, g: ((j * batch + b) * g_n + g, 0, 0)),
        out_shape=jax.ShapeDtypeStruct((2 * batch * g_n, n16, LANE), BF16),
        compiler_params=_params("parallel", "parallel", "parallel"),
        name="nsa_compress",
    )(x16, w1r, w1r, per, w2.astype(BF16))


def _nsa_cmp_select_kernel(q_ref, kc_ref, vc_ref, c2s_ref, o_ref, sel_ref, *, tq, scale):
    qi = pl.program_id(2)
    kc = kc_ref[0]
    vc = vc_ref[0]
    ncp = kc.shape[0]
    tpos = qi * tq + lax.broadcasted_iota(jnp.int32, (tq, 1), 0)
    cend = lax.broadcasted_iota(jnp.int32, (1, ncp), 1) * CMP_STRIDE + (CMP_BLOCK - 1)
    valid = cend <= tpos
    psum = jnp.zeros((tq, ncp), F32)
    for hh in range(NSA_HPG):
        s = lax.dot_general(q_ref[hh], kc, _NT, preferred_element_type=F32) * scale
        s = jnp.where(valid, s, NEG)
        m = jnp.max(s, axis=-1, keepdims=True)
        e = jnp.where(valid, jnp.exp(s - m), 0.0)
        l = jnp.sum(e, axis=-1, keepdims=True)
        p = e / jnp.where(l > 0.0, l, 1.0)
        o_ref[:, hh * LANE:(hh + 1) * LANE] = jnp.dot(p.astype(BF16), vc, preferred_element_type=F32)
        psum = psum + p
    imp = jnp.dot(psum.astype(BF16), c2s_ref[...], preferred_element_type=F32)
    blk = lax.broadcasted_iota(jnp.int32, imp.shape, 1)
    cur = tpos >> (SLC_BLOCK.bit_length() - 1)
    forced = (blk == 0) | (blk == cur) | (blk == cur - 1)
    imp = jnp.where(forced, FORCE, imp)
    imp = jnp.where(blk <= cur, imp, NEG)
    blk_f = blk.astype(F32)
    sel = jnp.zeros(imp.shape, jnp.bool_)
    x = imp
    for _ in range(SLC_TOPK):
        m = jnp.max(x, axis=-1, keepdims=True)
        idx = jnp.min(jnp.where(x == m, blk_f, float(LANE)), axis=-1, keepdims=True)
        hit = blk_f == idx
        sel = sel | (hit & (m > 0.5 * NEG))
        x = jnp.where(hit, -jnp.inf, x)
    sel_ref[0] = jnp.where(sel, 0.0, NEG).astype(sel_ref.dtype)


def _nsa_cmp_select(proj, kvc, c2s, *, batch, seq, tq=512):
    g_n, hpg = NSA_GROUPS, NSA_HPG
    t = batch * seq
    nq = seq // tq
    n16 = seq // CMP_STRIDE
    return pl.pallas_call(
        functools.partial(_nsa_cmp_select_kernel, tq=tq, scale=HEAD_DIM ** -0.5),
        grid=(batch, g_n, nq),
        in_specs=[pl.BlockSpec((hpg, tq, LANE), lambda b, g, i: (g, b * nq + i, 0)),
                  pl.BlockSpec((1, n16, LANE), lambda b, g, i: (b * g_n + g, 0, 0)),
                  pl.BlockSpec((1, n16, LANE), lambda b, g, i: ((batch + b) * g_n + g, 0, 0)),
                  pl.BlockSpec((n16, LANE), lambda b, g, i: (0, 0))],
        out_specs=[pl.BlockSpec((tq, hpg * LANE), lambda b, g, i: (b * nq + i, g)),
                   pl.BlockSpec((1, tq, LANE), lambda b, g, i: (g, b * nq + i, 0))],
        out_shape=[jax.ShapeDtypeStruct((t, g_n * hpg * LANE), F32),
                   jax.ShapeDtypeStruct((g_n, t, LANE), BF16)],
        compiler_params=_params("parallel", "parallel", "parallel"),
        name="nsa_compressed_select",
    )(proj, kvc, kvc, c2s)


def _nsa_selected_kernel(q_ref, sel_ref, k_ref, v_ref, et_ref, o_ref, qa_ref, m_ref, acc_ref,
                         *, tq, tk, scale):
    qi = pl.program_id(2)
    hpg = NSA_HPG
    for hh in range(hpg):
        qa_ref[hh * tq:(hh + 1) * tq, :LANE] = q_ref[hh]
        qa_ref[hh * tq:(hh + 1) * tq, LANE:] = sel_ref[0]
    _softmax_init(m_ref, acc_ref)

    def tile(kj, masked):
        k0 = pl.multiple_of(kj * tk, tk)
        ka = jnp.concatenate([k_ref[0, pl.ds(k0, tk), :], et_ref[pl.ds(k0, tk), :]], axis=1)
        v = v_ref[0, pl.ds(k0, tk), :]
        mask = _causal_mask(qi, tq, k0, tk) if masked else None
        for hh in range(hpg):
            rows = slice(hh * tq, (hh + 1) * tq)
            s = lax.dot_general(qa_ref[rows, :], ka, _NT, preferred_element_type=F32) * scale
            if masked:
                s = jnp.where(mask, s, NEG)
            _softmax_tile(s, v, m_ref, acc_ref, rows, mask)

    n_full = (qi * tq) // tk
    _loop_tiles(n_full, lambda kj: tile(kj, False), group=4)
    tile(n_full, True)
    for hh in range(hpg):
        o_ref[:, hh * LANE:(hh + 1) * LANE] = _softmax_result(acc_ref, slice(hh * tq, (hh + 1) * tq))


def _nsa_selected(proj, sel, et, *, batch, seq, q_chunk, k_chunk, v_chunk, tq=256, tk=512):
    g_n, hpg = NSA_GROUPS, NSA_HPG
    t = batch * seq
    nq = seq // tq
    assert tq & (tq - 1) == 0 and seq % tk == 0 and tk % tq == 0 and q_chunk % hpg == 0
    return pl.pallas_call(
        functools.partial(_nsa_selected_kernel, tq=tq, tk=tk, scale=HEAD_DIM ** -0.5),
        grid=(batch, g_n, nq),
        in_specs=[pl.BlockSpec((hpg, tq, LANE), lambda b, g, i: (q_chunk // hpg + g, b * nq + i, 0)),
                  pl.BlockSpec((1, tq, LANE), lambda b, g, i: (g, b * nq + i, 0)),
                  pl.BlockSpec((1, seq, LANE), lambda b, g, i: (k_chunk + g, b, 0)),
                  pl.BlockSpec((1, seq, LANE), lambda b, g, i: (v_chunk + g, b, 0)),
                  pl.BlockSpec((seq, LANE), lambda b, g, i: (0, 0))],
        out_specs=pl.BlockSpec((tq, hpg * LANE), lambda b, g, i: (b * nq + i, g)),
        out_shape=jax.ShapeDtypeStruct((t, g_n * hpg * LANE), F32),
        scratch_shapes=[pltpu.VMEM((hpg * tq, 2 * LANE), BF16), pltpu.VMEM((hpg * tq, LANE), F32),
                        pltpu.VMEM((hpg * tq, 2 * LANE), F32)],
        compiler_params=_params("parallel", "parallel", "arbitrary"),
        name="nsa_selected",
    )(proj, sel, proj, proj, et)


def _nsa_window_kernel(q_ref, k_ref, v_ref, o_ref, m_ref, acc_ref, *, tile, scale):
    qi = pl.program_id(2)
    hpg = NSA_HPG
    _softmax_init(m_ref, acc_ref)
    span = WINDOW + tile
    k0 = pl.multiple_of(jnp.maximum(qi * tile - WINDOW, 0), tile)
    k = k_ref[0, pl.ds(k0, span), :]
    v = v_ref[0, pl.ds(k0, span), :]
    qpos = qi * tile + lax.broadcasted_iota(jnp.int32, (tile, span), 0)
    kpos = k0 + lax.broadcasted_iota(jnp.int32, (tile, span), 1)
    mask = (kpos <= qpos) & (kpos > qpos - WINDOW)
    for hh in range(hpg):
        rows = slice(hh * tile, (hh + 1) * tile)
        s = lax.dot_general(q_ref[hh], k, _NT, preferred_element_type=F32) * scale
        _softmax_tile(jnp.where(mask, s, NEG), v, m_ref, acc_ref, rows, mask)
        o_ref[:, hh * LANE:(hh + 1) * LANE] = _softmax_result(acc_ref, rows)


def _nsa_window(proj, *, batch, seq, q_chunk, k_chunk, v_chunk, tile=256):
    g_n, hpg = NSA_GROUPS, NSA_HPG
    t = batch * seq
    nq = seq // tile
    assert tile & (tile - 1) == 0 and WINDOW % tile == 0 and q_chunk % hpg == 0 and seq >= WINDOW + tile
    return pl.pallas_call(
        functools.partial(_nsa_window_kernel, tile=tile, scale=HEAD_DIM ** -0.5),
        grid=(batch, g_n, nq),
        in_specs=[pl.BlockSpec((hpg, tile, LANE), lambda b, g, i: (q_chunk // hpg + g, b * nq + i, 0)),
                  pl.BlockSpec((1, seq, LANE), lambda b, g, i: (k_chunk + g, b, 0)),
                  pl.BlockSpec((1, seq, LANE), lambda b, g, i: (v_chunk + g, b, 0))],
        out_specs=pl.BlockSpec((tile, hpg * LANE), lambda b, g, i: (b * nq + i, g)),
        out_shape=jax.ShapeDtypeStruct((t, g_n * hpg * LANE), F32),
        scratch_shapes=[pltpu.VMEM((hpg * tile, LANE), F32), pltpu.VMEM((hpg * tile, 2 * LANE), F32)],
        compiler_params=_params("parallel", "parallel", "arbitrary"),
        name="nsa_window",
    )(proj, proj, proj)


def _nsa_out_kernel(oc_ref, os_ref, ow_ref, gate_ref, w_ref, h_ref, g_ref, b_ref, out_ref):
    gates = gate_ref[...]
    parts = []
    for hh in range(N_HEADS):
        cols = slice(hh * LANE, (hh + 1) * LANE)
        parts.append(gates[:, 3 * hh:3 * hh + 1] * oc_ref[:, cols]
                     + gates[:, 3 * hh + 1:3 * hh + 2] * os_ref[:, cols]
                     + gates[:, 3 * hh + 2:3 * hh + 3] * ow_ref[:, cols])
    o = jnp.concatenate(parts, axis=1).astype(BF16)
    y = jnp.dot(o, w_ref[...], preferred_element_type=F32)
    out_ref[...] = _layer_norm(DEEPNORM_ALPHA * h_ref[...] + y, g_ref[...], b_ref[...])


def _nsa_out_ln(o_cmp, o_slc, o_win, gates, w, h, g, b, *, tm=256):
    t, d = h.shape
    row = lambda i: (i, 0)
    fixed = lambda i: (0, 0)
    return pl.pallas_call(
        _nsa_out_kernel,
        grid=(t // tm,),
        in_specs=[pl.BlockSpec((tm, d), row), pl.BlockSpec((tm, d), row), pl.BlockSpec((tm, d), row),
                  pl.BlockSpec((tm, LANE), row), pl.BlockSpec((d, d), fixed), pl.BlockSpec((tm, d), row),
                  pl.BlockSpec((1, d), fixed), pl.BlockSpec((1, d), fixed)],
        out_specs=pl.BlockSpec((tm, d), row),
        out_shape=jax.ShapeDtypeStruct((t, d), F32),
        compiler_params=_params("parallel"),
        name="nsa_out_ln",
    )(o_cmp, o_slc, o_win, gates, w, h, g.reshape(1, d), b.reshape(1, d))


def _nsa_layer(h, w_in, cmp_pe, cmp_w1, cmp_w2, w_out, ln_g, ln_b, *, batch, seq):
    nh, g_n, d = N_HEADS, NSA_GROUPS, HEAD_DIM
    dm = nh * d
    n_slc = seq // SLC_BLOCK
    n16 = seq // CMP_STRIDE
    n_cmp = (seq - CMP_BLOCK) // CMP_STRIDE + 1
    assert n_slc <= LANE and n_cmp <= n16
    wq = w_in[:, :dm]
    wkv = w_in[:, dm:dm + 6 * g_n * d]
    wg = jnp.pad(w_in[:, dm + 6 * g_n * d:], ((0, 0), (0, LANE - 3 * nh)))
    n_cmp_chunks = 2 * g_n
    w_all = jnp.concatenate([wq, wq, wkv[:, n_cmp_chunks * d:], wkv[:, :n_cmp_chunks * d], wg], axis=1).astype(BF16)
    kv0 = 2 * nh
    n_main = kv0 + 4 * g_n
    rotated = set(range(nh, 2 * nh)) | set(range(kv0, kv0 + g_n)) | set(range(kv0 + 2 * g_n, kv0 + 3 * g_n))
    cos, sin = _rope_tables(seq, d)

    def epilogue(c, val, aux):
        if c in rotated:
            val = _rope_chunk(val, aux[0], aux[1], d)
        if c == n_main + n_cmp_chunks:
            val = _sigmoid(val)
        return val

    proj, cmp_src, gates = _project_heads(h, w_all, [n_main, n_cmp_chunks, 1], [BF16, BF16, F32], seq=seq,
                                          aux=(cos, sin), epilogue=epilogue)
    kvc = _nsa_compress(cmp_src, cmp_pe, cmp_w1, cmp_w2, batch=batch, seq=seq, first_chunk=0)
    cs = jnp.arange(n16) * CMP_STRIDE
    ss = jnp.arange(LANE) * SLC_BLOCK
    ov = jnp.clip(jnp.minimum(cs[:, None] + CMP_BLOCK, ss[None, :] + SLC_BLOCK)
                  - jnp.maximum(cs[:, None], ss[None, :]), 0, None) / CMP_BLOCK
    ov = jnp.where((jnp.arange(n16)[:, None] < n_cmp) & (jnp.arange(LANE)[None, :] < n_slc), ov, 0.0)
    c2s = ov.astype(BF16)
    et = (jnp.arange(seq)[:, None] // SLC_BLOCK == jnp.arange(LANE)[None, :]).astype(BF16)
    o_cmp, sel = _nsa_cmp_select(proj, kvc, c2s, batch=batch, seq=seq)
    o_slc = _nsa_selected(proj, sel, et, batch=batch, seq=seq, q_chunk=nh,
                          k_chunk=kv0, v_chunk=kv0 + g_n)
    o_win = _nsa_window(proj, batch=batch, seq=seq, q_chunk=nh,
                        k_chunk=kv0 + 2 * g_n, v_chunk=kv0 + 3 * g_n)
    return _nsa_out_ln(o_cmp, o_slc, o_win, gates[0], w_out.astype(BF16), h, ln_g, ln_b)


def _moe_expert_kernel(te_ref, nu_ref, x_ref, wg_ref, wu_ref, wd_ref, out_ref, acc_ref):
    i = pl.program_id(0)
    k = pl.program_id(1)
    last = pl.num_programs(1) - 1
    used = i < nu_ref[0]

    @pl.when(used)
    def _():
        @pl.when(k == 0)
        def _():
            acc_ref[...] = jnp.zeros_like(acc_ref)

        x = x_ref[...]
        gate = jnp.dot(x, wg_ref[0, 0].astype(BF16), preferred_element_type=F32)
        up = jnp.dot(x, wu_ref[0, 0].astype(BF16), preferred_element_type=F32)
        act = (gate * _sigmoid(gate) * up).astype(BF16)
        acc_ref[...] += jnp.dot(act, wd_ref[0, 0].astype(BF16), preferred_element_type=F32)

        @pl.when(k == last)
        def _():
            out_ref[...] = acc_ref[...]

    @pl.when(jnp.logical_not(used) & (k == last))
    def _():
        out_ref[...] = jnp.zeros_like(out_ref)


def _moe_experts(xs, tile_expert, n_used, wg, wu, wd, *, layer, tm, tf=512):
    n_slots, d = xs.shape
    ffe = wg.shape[3]
    assert n_slots % tm == 0 and ffe % tf == 0
    grid_spec = pltpu.PrefetchScalarGridSpec(
        num_scalar_prefetch=2,
        grid=(n_slots // tm, ffe // tf),
        in_specs=[pl.BlockSpec((tm, d), lambda i, k, te, nu: (i, 0)),
                  pl.BlockSpec((1, 1, d, tf), lambda i, k, te, nu: (layer, te[i], 0, k)),
                  pl.BlockSpec((1, 1, d, tf), lambda i, k, te, nu: (layer, te[i], 0, k)),
                  pl.BlockSpec((1, 1, tf, d), lambda i, k, te, nu: (layer, te[i], k, 0))],
        out_specs=pl.BlockSpec((tm, d), lambda i, k, te, nu: (i, 0)),
        scratch_shapes=[pltpu.VMEM((tm, d), F32)])
    return pl.pallas_call(
        _moe_expert_kernel,
        grid_spec=grid_spec,
        out_shape=jax.ShapeDtypeStruct((n_slots, d), F32),
        compiler_params=_params("parallel", "arbitrary"),
        name="moe_experts",
    )(tile_expert, n_used, xs, wg, wu, wd)


def _moe_combine_kernel(y0_ref, y1_ref, gate_ref, h_ref, g_ref, b_ref, out_ref):
    gates = gate_ref[...]
    y = gates[:, 0:1] * y0_ref[...] + gates[:, 1:2] * y1_ref[...]
    out_ref[...] = _layer_norm(DEEPNORM_ALPHA * h_ref[...] + y, g_ref[...], b_ref[...])


def _moe_combine_ln(y0, y1, gates, h, g, b, *, tm=512):
    t, d = h.shape
    row = lambda i: (i, 0)
    fixed = lambda i: (0, 0)
    return pl.pallas_call(
        _moe_combine_kernel,
        grid=(t // tm,),
        in_specs=[pl.BlockSpec((tm, d), row), pl.BlockSpec((tm, d), row), pl.BlockSpec((tm, TOP_K), row),
                  pl.BlockSpec((tm, d), row), pl.BlockSpec((1, d), fixed), pl.BlockSpec((1, d), fixed)],
        out_specs=pl.BlockSpec((tm, d), row),
        out_shape=jax.ShapeDtypeStruct((t, d), F32),
        compiler_params=_params("parallel"),
        name="moe_combine_ln",
    )(y0, y1, gates, h, g.reshape(1, d), b.reshape(1, d))


def _moe_layer(h, h_b, logits, wg, wu, wd, ln_g, ln_b, *, layer, tm=1024):
    t, d = h.shape
    ne = N_EXPERTS
    top_val, top_idx = lax.top_k(logits[:, :ne], TOP_K)
    gates = jax.nn.softmax(top_val, axis=-1)
    n_assign = t * TOP_K
    flat_e = top_idx.reshape(n_assign).astype(jnp.int32)
    flat_tok = jnp.repeat(jnp.arange(t, dtype=jnp.int32), TOP_K)
    onehot = (flat_e[:, None] == jnp.arange(ne, dtype=jnp.int32)[None, :]).astype(jnp.int32)
    before = jnp.cumsum(onehot, axis=0) - onehot
    rank = jnp.sum(before * onehot, axis=1)
    counts = jnp.sum(onehot, axis=0)
    padded = (counts + tm - 1) // tm * tm
    pad_end = jnp.cumsum(padded)
    pad_start = pad_end - padded
    grp_start = jnp.cumsum(counts) - counts
    dest = (pad_start[flat_e] + rank).astype(jnp.int32).reshape(t, TOP_K)
    n_tiles = -(-n_assign // tm) + ne
    tile_expert = jnp.minimum(jnp.searchsorted(pad_end, jnp.arange(n_tiles) * tm, side='right'),
                              ne - 1).astype(jnp.int32)
    n_used = (pad_end[-1] // tm).astype(jnp.int32).reshape(1)
    sorted_tok = flat_tok[jnp.argsort(flat_e, stable=True)]
    slot_e = jnp.repeat(tile_expert, tm)
    local = jnp.arange(n_tiles * tm, dtype=jnp.int32) - pad_start[slot_e].astype(jnp.int32)
    real = local < counts[slot_e]
    src = jnp.clip(grp_start[slot_e].astype(jnp.int32) + local, 0, n_assign - 1)
    slot_tok = jnp.where(real, sorted_tok[src], 0)
    xs = h_b[slot_tok]
    ys = _moe_experts(xs, tile_expert, n_used, wg, wu, wd, layer=layer, tm=tm)
    return _moe_combine_ln(ys[dest[:, 0]], ys[dest[:, 1]], gates, h, ln_g, ln_b)


def kernel(x, hg_w_in, hg_lb, hg_norm_g, hg_w_out, da_w_in, da_lam, da_norm_g, da_w_out, nsa_w_in, nsa_cmp_pe, nsa_cmp_w1, nsa_cmp_w2, nsa_w_out, sb_w_in, sb_w_out, ffn_w_gate, ffn_w_up, ffn_w_down, moe_w_router, moe_w_gate, moe_w_up, moe_w_down, ln_g, ln_b):
    batch, seq, d = x.shape
    h = x.reshape(batch * seq, d)

    def router(j):
        return jnp.pad(moe_w_router[j], ((0, 0), (0, LANE - N_EXPERTS))).astype(BF16)

    def dense_ffn(h, j, layer):
        return _ffn_ln(h, ffn_w_gate[j].astype(BF16), ffn_w_up[j].astype(BF16), ffn_w_down[j].astype(BF16),
                       ln_g[layer, 1], ln_b[layer, 1])

    def expert_ffn(h, h_b, logits, j, layer):
        return _moe_layer(h, h_b, logits, moe_w_gate, moe_w_up, moe_w_down, ln_g[layer, 1], ln_b[layer, 1], layer=j)

    h = _hgrn2_layer(h, hg_w_in, hg_lb, hg_norm_g, hg_w_out, ln_g[0, 0], ln_b[0, 0],
                     batch=batch, seq=seq, layer=0)
    h = dense_ffn(h, 0, 0)
    h, h_b, logits = _diff_attention_layer(h, da_w_in, da_lam, da_norm_g, da_w_out, ln_g[1, 0], ln_b[1, 0],
                                           router(0), batch=batch, seq=seq, layer=1)
    h = expert_ffn(h, h_b, logits, 0, 1)
    h = _nsa_layer(h, nsa_w_in, nsa_cmp_pe, nsa_cmp_w1, nsa_cmp_w2, nsa_w_out, ln_g[2, 0], ln_b[2, 0],
                   batch=batch, seq=seq)
    h = dense_ffn(h, 1, 2)
    h, h_b, logits = _stick_breaking_layer(h, sb_w_in, sb_w_out, ln_g[3, 0], ln_b[3, 0], router(1),
                                           batch=batch, seq=seq)
    h = expert_ffn(h, h_b, logits, 1, 3)
    return h.reshape(batch, seq, d)
```

```python
import functools
import math

import jax
import jax.numpy as jnp
from jax import lax
from jax.experimental import pallas as pl
from jax.experimental.pallas import tpu as pltpu

F32 = jnp.float32
BF16 = jnp.bfloat16

LANE = 128
VMEM_LIMIT_BYTES = 56 * 1024 * 1024

N_HEADS = 8
HEAD_DIM = 128
ROPE_THETA = 10000.0
HG_CHUNK = 64
HG_SUB = 16
DA_HALF = 64
NSA_GROUPS = 2
NSA_HPG = 4
CMP_BLOCK = 32
CMP_STRIDE = 16
SLC_BLOCK = 64
SLC_TOPK = 16
WINDOW = 512
N_EXPERTS = 8
TOP_K = 2
LN_EPS = 1e-5
NEG = -1e30
FORCE = 1e9
DEPTH = 4
DEEPNORM_ALPHA = (2 * DEPTH) ** 0.25

_NT = (((1,), (1,)), ((), ()))
_EXP_F32_ZERO_BELOW = -104.0

def _params(*semantics):
    return pltpu.CompilerParams(dimension_semantics=semantics, vmem_limit_bytes=VMEM_LIMIT_BYTES)


def _sigmoid(x):
    return 1.0 / (1.0 + jnp.exp(-x))


def _layer_norm(z, g, b):
    mu = jnp.mean(z, axis=-1, keepdims=True)
    zc = z - mu
    var = jnp.mean(zc * zc, axis=-1, keepdims=True)
    return zc * lax.rsqrt(var + LN_EPS) * g + b


def _proj_kernel(*refs, n_aux, groups, epilogue):
    x_ref, w_ref = refs[0], refs[1]
    aux_refs = refs[2:2 + n_aux]
    out_refs = refs[2 + n_aux:]
    x = x_ref[...].astype(BF16)
    aux = [r[...] for r in aux_refs]
    dest = [(out_ref, local) for out_ref, n in zip(out_refs, groups) for local in range(n)]
    width = 2
    for c0 in range(0, len(dest), width):
        n = min(width, len(dest) - c0)
        acc = jnp.dot(x, w_ref[:, c0 * LANE:(c0 + n) * LANE], preferred_element_type=F32)
        for c in range(c0, c0 + n):
            val = acc[:, (c - c0) * LANE:(c - c0 + 1) * LANE]
            if epilogue is not None:
                val = epilogue(c, val, aux)
            out_ref, local = dest[c]
            out_ref[local] = val.astype(out_ref.dtype)


def _project_heads(x, w, groups, dtypes, *, seq, aux=(), epilogue=None, tm=512):
    t, d = x.shape
    n_chunks = sum(groups)
    assert w.shape == (d, n_chunks * LANE) and t % tm == 0 and seq % tm == 0
    tiles_per_seq = seq // tm
    in_specs = [pl.BlockSpec((tm, d), lambda i: (i, 0)),
                pl.BlockSpec((d, n_chunks * LANE), lambda i: (0, 0))]
    for a in aux:
        assert a.shape == (seq, LANE)
        in_specs.append(pl.BlockSpec((tm, LANE), lambda i: (i % tiles_per_seq, 0)))
    out_shape = [jax.ShapeDtypeStruct((n, t, LANE), dt) for n, dt in zip(groups, dtypes)]
    out_specs = [pl.BlockSpec((n, tm, LANE), lambda i: (0, i, 0)) for n in groups]
    return pl.pallas_call(
        functools.partial(_proj_kernel, n_aux=len(aux), groups=tuple(groups), epilogue=epilogue),
        grid=(t // tm,),
        in_specs=in_specs,
        out_specs=out_specs,
        out_shape=out_shape,
        compiler_params=_params("parallel"),
        name="project_heads",
    )(x, w, *aux)


def _outproj_ln_kernel(*refs, with_router):
    if with_router:
        o_ref, w_ref, h_ref, g_ref, b_ref, wr_ref, out_ref, outb_ref, logit_ref = refs
    else:
        o_ref, w_ref, h_ref, g_ref, b_ref, out_ref = refs
    y = jnp.dot(o_ref[...], w_ref[...], preferred_element_type=F32)
    hn = _layer_norm(DEEPNORM_ALPHA * h_ref[...] + y, g_ref[...], b_ref[...])
    out_ref[...] = hn
    if with_router:
        hb = hn.astype(BF16)
        outb_ref[...] = hb
        logit_ref[...] = jnp.dot(hb, wr_ref[...], preferred_element_type=F32)


def _outproj_ln(o, w, h, g, b, w_router=None, *, tm=512):
    t, d = h.shape
    assert o.shape == (t, d) and t % tm == 0
    with_router = w_router is not None
    row = lambda i: (i, 0)
    fixed = lambda i: (0, 0)
    in_specs = [pl.BlockSpec((tm, d), row), pl.BlockSpec((d, d), fixed), pl.BlockSpec((tm, d), row),
                pl.BlockSpec((1, d), fixed), pl.BlockSpec((1, d), fixed)]
    args = [o, w, h, g.reshape(1, d), b.reshape(1, d)]
    out_shape = [jax.ShapeDtypeStruct((t, d), F32)]
    out_specs = [pl.BlockSpec((tm, d), row)]
    if with_router:
        in_specs.append(pl.BlockSpec((d, LANE), fixed))
        args.append(w_router)
        out_shape += [jax.ShapeDtypeStruct((t, d), BF16), jax.ShapeDtypeStruct((t, LANE), F32)]
        out_specs += [pl.BlockSpec((tm, d), row), pl.BlockSpec((tm, LANE), row)]
    res = pl.pallas_call(
        functools.partial(_outproj_ln_kernel, with_router=with_router),
        grid=(t // tm,),
        in_specs=in_specs,
        out_specs=out_specs,
        out_shape=out_shape,
        compiler_params=_params("parallel"),
        name="outproj_ln",
    )(*args)
    return res if with_router else res[0]


def _ffn_kernel(h_ref, wg_ref, wu_ref, wd_ref, g_ref, b_ref, out_ref):
    x = h_ref[...].astype(BF16)
    gate = jnp.dot(x, wg_ref[...], preferred_element_type=F32)
    up = jnp.dot(x, wu_ref[...], preferred_element_type=F32)
    act = (gate * _sigmoid(gate) * up).astype(BF16)
    y = jnp.dot(act, wd_ref[...], preferred_element_type=F32)
    out_ref[...] = _layer_norm(DEEPNORM_ALPHA * h_ref[...] + y, g_ref[...], b_ref[...])


def _ffn_ln(h, wg, wu, wd, g, b, *, tm=512):
    t, d = h.shape
    ff = wg.shape[1]
    assert t % tm == 0
    row = lambda i: (i, 0)
    fixed = lambda i: (0, 0)
    resident = pl.Buffered(1)
    return pl.pallas_call(
        _ffn_kernel,
        grid=(t // tm,),
        in_specs=[pl.BlockSpec((tm, d), row),
                  pl.BlockSpec((d, ff), fixed, pipeline_mode=resident),
                  pl.BlockSpec((d, ff), fixed, pipeline_mode=resident),
                  pl.BlockSpec((ff, d), fixed, pipeline_mode=resident),
                  pl.BlockSpec((1, d), fixed), pl.BlockSpec((1, d), fixed)],
        out_specs=pl.BlockSpec((tm, d), row),
        out_shape=jax.ShapeDtypeStruct((t, d), F32),
        compiler_params=_params("parallel"),
        name="swiglu_ln",
    )(h, wg, wu, wd, g.reshape(1, d), b.reshape(1, d))


def _hgrn_kernel(lb_ref, ng_ref, q_ref, f_ref, i_ref, g_ref, o_ref, state_ref, *, rows, hp):
    c, sub = HG_CHUNK, HG_SUB

    @pl.when(pl.program_id(2) == 0)
    def _():
        state_ref[...] = jnp.zeros_like(state_ref)

    ng = ng_ref[...]
    r_io = lax.broadcasted_iota(jnp.int32, (c, c), 0)
    c_io = lax.broadcasted_iota(jnp.int32, (c, c), 1)
    tri = (r_io >= c_io).astype(BF16)
    row_c = lax.broadcasted_iota(jnp.int32, (c, LANE), 0)
    row_s = lax.broadcasted_iota(jnp.int32, (sub, LANE), 0)

    def head_chunk(hh, r0):
        lb = lb_ref[hh]
        qr = q_ref[hh, pl.ds(r0, c), :]
        fr = f_ref[hh, pl.ds(r0, c), :]
        v = i_ref[hh, pl.ds(r0, c), :]
        gr = g_ref[hh, pl.ds(r0, c), :]
        forget = lb + (1.0 - lb) * _sigmoid(fr)
        lf = jnp.log(forget)
        k = 1.0 - forget
        q = qr * _sigmoid(qr)
        lf_hi = lf.astype(BF16)
        lf_lo = (lf - lf_hi.astype(F32)).astype(BF16)
        cum = (jnp.dot(tri, lf_hi, preferred_element_type=F32)
               + jnp.dot(tri, lf_lo, preferred_element_type=F32))
        v_b = v.astype(BF16)
        state_t = state_ref[hh]

        qe = (q * jnp.exp(cum)).astype(BF16)
        out = lax.dot_general(qe, state_t.astype(BF16), _NT, preferred_element_type=F32)

        a_rows = [jnp.zeros((sub, c), F32)]
        for i in range(1, c // sub):
            anchor = cum[i * sub - 1:i * sub, :]
            kd = jnp.where(row_c < i * sub, k * jnp.exp(jnp.minimum(anchor - cum, 0.0)), 0.0)
            qd = q[i * sub:(i + 1) * sub] * jnp.exp(cum[i * sub:(i + 1) * sub] - anchor)
            a_rows.append(lax.dot_general(qd.astype(BF16), kd.astype(BF16), _NT,
                                          preferred_element_type=F32))
        a_off = jnp.concatenate(a_rows, axis=0).astype(BF16)
        out = out + jnp.dot(a_off, v_b, preferred_element_type=F32)

        diag = []
        for i in range(c // sub):
            sl = slice(i * sub, (i + 1) * sub)
            f_i, q_i, k_i, v_i = forget[sl], q[sl], k[sl], v[sl]
            o_i = jnp.zeros((sub, LANE), F32)
            decay = jnp.zeros((sub, LANE), F32)
            for s in range(sub - 1, -1, -1):
                carried = decay * f_i[s + 1:s + 2, :] if s + 1 < sub else decay
                decay = jnp.where(row_s == s, 1.0, carried)
                w = q_i * decay * k_i[s:s + 1, :]
                o_i = o_i + jnp.sum(w, axis=-1, keepdims=True) * v_i[s:s + 1, :]
            diag.append(o_i)
        out = out + jnp.concatenate(diag, axis=0)

        last = cum[c - 1:c, :]
        kdl = (k * jnp.exp(last - cum)).astype(BF16)
        state_ref[hh] = jnp.exp(last) * state_t + jnp.dot(v.T.astype(BF16), kdl,
                                                           preferred_element_type=F32)

        ms = jnp.mean(out * out, axis=-1, keepdims=True)
        o_ref[pl.ds(r0, c), hh * LANE:(hh + 1) * LANE] = (
            out * lax.rsqrt(ms + LN_EPS) * ng * _sigmoid(gr)).astype(o_ref.dtype)

    def chunk(ci, carry):
        r0 = pl.multiple_of(ci * c, c)
        for hh in range(hp):
            head_chunk(hh, r0)
        return carry

    lax.fori_loop(0, rows // c, chunk, 0)


def _hgrn_core(proj, lb, norm_g, *, batch, seq, rows=512, hp=4):
    nh = N_HEADS
    t = batch * seq
    assert seq % rows == 0 and rows % HG_CHUNK == 0 and nh % hp == 0
    spb = seq // rows
    ng = nh // hp

    def head_spec(offset):
        return pl.BlockSpec((hp, rows, LANE), lambda b, h, s: (offset + h, b * spb + s, 0))

    return pl.pallas_call(
        functools.partial(_hgrn_kernel, rows=rows, hp=hp),
        grid=(batch, ng, spb),
        in_specs=[pl.BlockSpec((hp, 1, LANE), lambda b, h, s: (h, 0, 0)),
                  pl.BlockSpec((1, LANE), lambda b, h, s: (0, 0)),
                  head_spec(0), head_spec(ng), head_spec(2 * ng), head_spec(3 * ng)],
        out_specs=pl.BlockSpec((rows, hp * LANE), lambda b, h, s: (b * spb + s, h)),
        out_shape=jax.ShapeDtypeStruct((t, nh * LANE), BF16),
        scratch_shapes=[pltpu.VMEM((hp, LANE, LANE), F32)],
        compiler_params=_params("parallel", "parallel", "arbitrary"),
        name="hgrn2_core",
    )(lb.reshape(nh, 1, LANE), norm_g.reshape(1, LANE), proj, proj, proj, proj)


def _hgrn2_layer(h, w_in, lb_logits, norm_g, w_out, ln_g, ln_b, *, batch, seq, layer):
    lb = jnp.cumsum(jax.nn.softmax(lb_logits.astype(F32), axis=0), axis=0)[layer]
    (proj,) = _project_heads(h, w_in.astype(BF16), [4 * N_HEADS], [F32], seq=seq)
    o = _hgrn_core(proj, lb, norm_g, batch=batch, seq=seq)
    return _outproj_ln(o, w_out.astype(BF16), h, ln_g, ln_b)


def _rope_tables(seq, rot_dim):
    half = rot_dim // 2
    inv = ROPE_THETA ** (-jnp.arange(half, dtype=F32) / half)
    ang = jnp.arange(seq).astype(F32)[:, None] * inv[None, :]
    reps = LANE // rot_dim
    cos = jnp.tile(jnp.concatenate([jnp.cos(ang), jnp.cos(ang)], axis=1), (1, reps))
    sin = jnp.tile(jnp.concatenate([-jnp.sin(ang), jnp.sin(ang)], axis=1), (1, reps))
    return cos, sin


def _rope_chunk(val, cos, sin, rot_dim):
    half = rot_dim // 2
    if rot_dim == LANE:
        partner = pltpu.roll(val, half, 1)
    else:
        lane = lax.broadcasted_iota(jnp.int32, val.shape, 1)
        partner = jnp.where(lane % rot_dim < half,
                            pltpu.roll(val, LANE - half, 1), pltpu.roll(val, half, 1))
    return val * cos + partner * sin


def _softmax_init(m_ref, acc_ref):
    m_ref[...] = jnp.full_like(m_ref, NEG)
    acc_ref[...] = jnp.zeros_like(acc_ref)


def _softmax_tile(s, v, m_ref, acc_ref, rows, mask=None):
    n_lt = s.shape[1] // LANE
    part = s[:, :LANE]
    for c in range(1, n_lt):
        part = jnp.maximum(part, s[:, c * LANE:(c + 1) * LANE])
    m_prev = m_ref[rows, :]
    m_new = jnp.maximum(m_prev, jnp.max(part, axis=-1, keepdims=True))
    alpha = jnp.exp(m_prev - m_new)
    ps = []
    for c in range(n_lt):
        cols = slice(c * LANE, (c + 1) * LANE)
        pc = jnp.exp(s[:, cols] - m_new)
        if mask is not None:
            pc = jnp.where(mask[:, cols], pc, 0.0)
        ps.append(pc.astype(BF16))
    p = jnp.concatenate(ps, axis=1)
    v_aug = jnp.concatenate([v, jnp.ones_like(v)], axis=1)
    acc_ref[rows, :] = (jnp.concatenate([alpha, alpha], axis=1) * acc_ref[rows, :]
                        + jnp.dot(p, v_aug, preferred_element_type=F32))
    m_ref[rows, :] = m_new


def _softmax_result(acc_ref, rows):
    return acc_ref[rows, :LANE] / acc_ref[rows, LANE:]


def _loop_tiles(n, step, group=2):
    def body(j, carry):
        for u in range(group):
            step(group * j + u)
        return carry

    lax.fori_loop(0, n // group, body, 0)
    done = n // group * group
    if group == 4:
        @pl.when(n - done >= 2)
        def _():
            step(done)
            step(done + 1)

    @pl.when(n % 2 == 1)
    def _():
        step(n - 1)


def _loop_tiles_while(n, step, alive):
    def body(carry):
        step(carry[0])
        return carry[0] + 1, alive()

    lax.while_loop(lambda c: (c[0] < n) & c[1], body, (jnp.int32(0), alive()))


def _causal_mask(qi, tq, k0, tk):
    qpos = qi * tq + lax.broadcasted_iota(jnp.int32, (tq, tk), 0)
    kpos = k0 + lax.broadcasted_iota(jnp.int32, (tq, tk), 1)
    return kpos <= qpos


def _diff_attn_kernel(lmb_ref, ng_ref, q_ref, k_ref, v_ref, o_ref, m_ref, acc_ref, *, tq, tk, hp, out_scale):
    qi = pl.program_id(2)
    lane = lax.broadcasted_iota(jnp.int32, (tq, LANE), 1)
    chains = []
    for hh in range(hp):
        q = q_ref[hh]
        zero = jnp.zeros_like(q)
        chains.append((hh, jnp.where(lane < DA_HALF, q, zero)))
        chains.append((hh, jnp.where(lane >= DA_HALF, q, zero)))
    _softmax_init(m_ref, acc_ref)

    def tile(kj, masked):
        k0 = pl.multiple_of(kj * tk, tk)
        mask = _causal_mask(qi, tq, k0, tk) if masked else None
        for c, (hh, qc) in enumerate(chains):
            s = lax.dot_general(qc, k_ref[hh, pl.ds(k0, tk), :], _NT, preferred_element_type=F32)
            if masked:
                s = jnp.where(mask, s, NEG)
            _softmax_tile(s, v_ref[hh, pl.ds(k0, tk), :], m_ref, acc_ref, slice(c * tq, (c + 1) * tq), mask)

    n_full = (qi * tq) // tk
    _loop_tiles(n_full, lambda kj: tile(kj, False), group=4)
    tile(n_full, True)

    for hh in range(hp):
        o = (_softmax_result(acc_ref, slice(2 * hh * tq, (2 * hh + 1) * tq))
             - lmb_ref[0] * _softmax_result(acc_ref, slice((2 * hh + 1) * tq, (2 * hh + 2) * tq)))
        ms = jnp.mean(o * o, axis=-1, keepdims=True)
        o_ref[:, hh * LANE:(hh + 1) * LANE] = (o * lax.rsqrt(ms + LN_EPS) * ng_ref[...] * out_scale).astype(o_ref.dtype)


def _diff_attn_core(qkv, lmb, norm_g, *, batch, seq, lam_init, tq=256, tk=512, hp=2):
    nh = N_HEADS
    t = batch * seq
    assert seq % tk == 0 and tk % tq == 0 and nh % hp == 0
    nq = seq // tq
    ng = nh // hp
    return pl.pallas_call(
        functools.partial(_diff_attn_kernel, tq=tq, tk=tk, hp=hp, out_scale=1.0 - lam_init),
        grid=(batch, ng, nq),
        in_specs=[pl.BlockSpec(memory_space=pltpu.SMEM),
                  pl.BlockSpec((1, LANE), lambda b, h, i: (0, 0)),
                  pl.BlockSpec((hp, tq, LANE), lambda b, h, i: (h, b * nq + i, 0)),
                  pl.BlockSpec((hp, seq, LANE), lambda b, h, i: (ng + h, b, 0)),
                  pl.BlockSpec((hp, seq, LANE), lambda b, h, i: (2 * ng + h, b, 0))],
        out_specs=pl.BlockSpec((tq, hp * LANE), lambda b, h, i: (b * nq + i, h)),
        out_shape=jax.ShapeDtypeStruct((t, nh * LANE), BF16),
        scratch_shapes=[pltpu.VMEM((2 * hp * tq, LANE), F32), pltpu.VMEM((2 * hp * tq, 2 * LANE), F32)],
        compiler_params=_params("parallel", "parallel", "arbitrary"),
        name="diff_attention",
    )(lmb.reshape(1), norm_g.reshape(1, LANE), qkv, qkv, qkv)


def _diff_attention_layer(h, w_in, lam, norm_g, w_out, ln_g, ln_b, w_router, *, batch, seq, layer):
    nh = N_HEADS
    lam_init = 0.8 - 0.6 * math.exp(-0.3 * layer)
    lf = lam.astype(F32)
    lmb = jnp.exp(jnp.sum(lf[0] * lf[1])) - jnp.exp(jnp.sum(lf[2] * lf[3])) + lam_init
    cos, sin = _rope_tables(seq, DA_HALF)
    q_scale = DA_HALF ** -0.5

    def epilogue(c, val, aux):
        if c < 2 * nh:
            val = _rope_chunk(val, aux[0], aux[1], DA_HALF)
        if c < nh:
            val = val * q_scale
        return val

    (qkv,) = _project_heads(h, w_in.astype(BF16), [3 * nh], [BF16], seq=seq, aux=(cos, sin), epilogue=epilogue)
    o = _diff_attn_core(qkv, lmb, norm_g, batch=batch, seq=seq, lam_init=lam_init)
    return _outproj_ln(o, w_out.astype(BF16), h, ln_g, ln_b, w_router)


def _stick_kernel(q_ref, k_ref, v_ref, o_ref, acc_ref, run_ref, *, tq, tk, sub, hp, scale):
    qi = pl.program_id(2)
    acc_ref[...] = jnp.zeros_like(acc_ref)
    run_ref[...] = jnp.zeros_like(run_ref)
    later = (lax.broadcasted_iota(jnp.int32, (sub, sub), 0)
             > lax.broadcasted_iota(jnp.int32, (sub, sub), 1)).astype(BF16)

    def tile(kj, masked):
        k0 = pl.multiple_of(kj * tk, tk)
        if masked:
            qpos = qi * tq + lax.broadcasted_iota(jnp.int32, (tq, tk), 0)
            strict = k0 + lax.broadcasted_iota(jnp.int32, (tq, tk), 1) < qpos
        for hh in range(hp):
            rows = slice(hh * tq, (hh + 1) * tq)
            z = lax.dot_general(q_ref[hh], k_ref[hh, pl.ds(k0, tk), :], _NT, preferred_element_type=F32) * scale
            neg_abs = pltpu.bitcast(pltpu.bitcast(z, jnp.uint32) | jnp.uint32(0x80000000), F32)
            soft = jnp.log(1.0 + jnp.exp(neg_abs))
            log_beta = jnp.minimum(z, 0.0) - soft
            log_keep = log_beta - z
            if masked:
                log_keep = jnp.where(strict, log_keep, 0.0)
            run = run_ref[rows, :]
            parts = [None] * (tk // sub)
            for si in reversed(range(tk // sub)):
                cols = slice(si * sub, (si + 1) * sub)
                lk = log_keep[:, cols]
                after = (jnp.dot(lk.astype(BF16), later, preferred_element_type=F32)
                         + jnp.concatenate([run] * (sub // LANE), axis=1))
                a = jnp.exp(log_beta[:, cols] + after)
                if masked:
                    a = jnp.where(strict[:, cols], a, 0.0)
                parts[si] = a.astype(BF16)
                run = run + jnp.sum(lk, axis=-1, keepdims=True)
            acc_ref[rows, :] += jnp.dot(jnp.concatenate(parts, axis=1), v_ref[hh, pl.ds(k0, tk), :],
                                        preferred_element_type=F32)
            run_ref[rows, :] = run

    n_full = (qi * tq) // tk
    tile(n_full, True)

    _loop_tiles_while(n_full, lambda j: tile(n_full - 1 - j, False),
                      alive=lambda: jnp.max(run_ref[...]) > _EXP_F32_ZERO_BELOW)
    for hh in range(hp):
        o_ref[:, hh * LANE:(hh + 1) * LANE] = acc_ref[hh * tq:(hh + 1) * tq, :].astype(o_ref.dtype)


def _stick_core(qkv, *, batch, seq, tq=256, tk=512, sub=256, hp=4):
    nh = N_HEADS
    t = batch * seq
    assert seq % tk == 0 and tk % tq == 0 and tk % sub == 0 and sub % LANE == 0 and nh % hp == 0
    nq = seq // tq
    ng = nh // hp
    return pl.pallas_call(
        functools.partial(_stick_kernel, tq=tq, tk=tk, sub=sub, hp=hp, scale=HEAD_DIM ** -0.5),
        grid=(batch, ng, nq),
        in_specs=[pl.BlockSpec((hp, tq, LANE), lambda b, h, i: (h, b * nq + i, 0)),
                  pl.BlockSpec((hp, seq, LANE), lambda b, h, i: (ng + h, b, 0)),
                  pl.BlockSpec((hp, seq, LANE), lambda b, h, i: (2 * ng + h, b, 0))],
        out_specs=pl.BlockSpec((tq, hp * LANE), lambda b, h, i: (b * nq + i, h)),
        out_shape=jax.ShapeDtypeStruct((t, nh * LANE), BF16),
        scratch_shapes=[pltpu.VMEM((hp * tq, LANE), F32), pltpu.VMEM((hp * tq, LANE), F32)],
        compiler_params=_params("parallel", "parallel", "arbitrary"),
        name="stick_breaking",
    )(qkv, qkv, qkv)


def _stick_breaking_layer(h, w_in, w_out, ln_g, ln_b, w_router, *, batch, seq):
    (qkv,) = _project_heads(h, w_in.astype(BF16), [3 * N_HEADS], [BF16], seq=seq)
    o = _stick_core(qkv, batch=batch, seq=seq)
    return _outproj_ln(o, w_out.astype(BF16), h, ln_g, ln_b, w_router)


def _nsa_compress_kernel(x_ref, w1a_ref, w1b_ref, pe_ref, w2_ref, o_ref):
    x = x_ref[0]
    first = jnp.dot(x, w1a_ref[0, 0], preferred_element_type=F32)
    second = jnp.dot(x, w1b_ref[0, 0], preferred_element_type=F32)
    n16 = x.shape[0]
    second = pltpu.roll(second, n16 - 1, 0)
    pe = jnp.broadcast_to(pe_ref[0], (8, pe_ref.shape[2])).astype(BF16)
    bias = (jnp.dot(pe[:, :x.shape[1]], w1a_ref[0, 0], preferred_element_type=F32)
            + jnp.dot(pe[:, x.shape[1]:], w1b_ref[0, 0], preferred_element_type=F32))
    pre = first + second + bias[0:1, :]
    hid = jax.nn.gelu(pre)
    o_ref[0] = jnp.dot(hid.astype(BF16), w2_ref[0], preferred_element_type=F32).astype(o_ref.dtype)


def _nsa_compress(proj, pe, w1, w2, *, batch, seq, first_chunk):
    g_n = NSA_GROUPS
    n16 = seq // CMP_STRIDE
    half = CMP_STRIDE * LANE
    nch = proj.shape[0]
    x16 = proj.reshape(nch, batch * n16, half)
    w1r = w1.astype(BF16).reshape(2, 2, half, LANE)
    per = pe.astype(F32).reshape(2, 1, 2 * half)
    return pl.pallas_call(
        _nsa_compress_kernel,
        grid=(2, batch, g_n),
        in_specs=[pl.BlockSpec((1, n16, half), lambda j, b, g: (first_chunk + j * g_n + g, b, 0)),
                  pl.BlockSpec((1, 1, half, LANE), lambda j, b, g: (j, 0, 0, 0)),
                  pl.BlockSpec((1, 1, half, LANE), lambda j, b, g: (j, 1, 0, 0)),
                  pl.BlockSpec((1, 1, 2 * half), lambda j, b, g: (j, 0, 0)),
                  pl.BlockSpec((1, LANE, LANE), lambda j, b, g: (j, 0, 0))],
        out_specs=pl.BlockSpec((1, n16, LANE), lambda j, b, g: ((j * batch + b) * g_n + g, 0, 0)),
        out_shape=jax.ShapeDtypeStruct((2 * batch * g_n, n16, LANE), BF16),
        compiler_params=_params("parallel", "parallel", "parallel"),
        name="nsa_compress",
    )(x16, w1r, w1r, per, w2.astype(BF16))


def _nsa_cmp_select_kernel(q_ref, kc_ref, vc_ref, c2s_ref, o_ref, sel_ref, *, tq, scale):
    qi = pl.program_id(2)
    kc = kc_ref[0]
    vc = vc_ref[0]
    ncp = kc.shape[0]
    tpos = qi * tq + lax.broadcasted_iota(jnp.int32, (tq, 1), 0)
    cend = lax.broadcasted_iota(jnp.int32, (1, ncp), 1) * CMP_STRIDE + (CMP_BLOCK - 1)
    valid = cend <= tpos
    psum = jnp.zeros((tq, ncp), F32)
    for hh in range(NSA_HPG):
        s = lax.dot_general(q_ref[hh], kc, _NT, preferred_element_type=F32) * scale
        s = jnp.where(valid, s, NEG)
        m = jnp.max(s, axis=-1, keepdims=True)
        e = jnp.where(valid, jnp.exp(s - m), 0.0)
        l = jnp.sum(e, axis=-1, keepdims=True)
        p = e / jnp.where(l > 0.0, l, 1.0)
        o_ref[:, hh * LANE:(hh + 1) * LANE] = jnp.dot(p.astype(BF16), vc, preferred_element_type=F32)
        psum = psum + p
    imp = jnp.dot(psum.astype(BF16), c2s_ref[...], preferred_element_type=F32)
    blk = lax.broadcasted_iota(jnp.int32, imp.shape, 1)
    cur = tpos >> (SLC_BLOCK.bit_length() - 1)
    forced = (blk == 0) | (blk == cur) | (blk == cur - 1)
    imp = jnp.where(forced, FORCE, imp)
    imp = jnp.where(blk <= cur, imp, NEG)
    blk_f = blk.astype(F32)
    sel = jnp.zeros(imp.shape, jnp.bool_)
    x = imp
    for _ in range(SLC_TOPK):
        m = jnp.max(x, axis=-1, keepdims=True)
        idx = jnp.min(jnp.where(x == m, blk_f, float(LANE)), axis=-1, keepdims=True)
        hit = blk_f == idx
        sel = sel | (hit & (m > 0.5 * NEG))
        x = jnp.where(hit, -jnp.inf, x)
    sel_ref[0] = jnp.where(sel, 0.0, NEG).astype(sel_ref.dtype)


def _nsa_cmp_select(proj, kvc, c2s, *, batch, seq, tq=512):
    g_n, hpg = NSA_GROUPS, NSA_HPG
    t = batch * seq
    nq = seq // tq
    n16 = seq // CMP_STRIDE
    return pl.pallas_call(
        functools.partial(_nsa_cmp_select_kernel, tq=tq, scale=HEAD_DIM ** -0.5),
        grid=(batch, g_n, nq),
        in_specs=[pl.BlockSpec((hpg, tq, LANE), lambda b, g, i: (g, b * nq + i, 0)),
                  pl.BlockSpec((1, n16, LANE), lambda b, g, i: (b * g_n + g, 0, 0)),
                  pl.BlockSpec((1, n16, LANE), lambda b, g, i: ((batch + b) * g_n + g, 0, 0)),
                  pl.BlockSpec((n16, LANE), lambda b, g, i: (0, 0))],
        out_specs=[pl.BlockSpec((tq, hpg * LANE), lambda b, g, i: (b * nq + i, g)),
                   pl.BlockSpec((1, tq, LANE), lambda b, g, i: (g, b * nq + i, 0))],
        out_shape=[jax.ShapeDtypeStruct((t, g_n * hpg * LANE), F32),
                   jax.ShapeDtypeStruct((g_n, t, LANE), BF16)],
        compiler_params=_params("parallel", "parallel", "parallel"),
        name="nsa_compressed_select",
    )(proj, kvc, kvc, c2s)


def _nsa_selected_kernel(q_ref, sel_ref, k_ref, v_ref, et_ref, o_ref, qa_ref, m_ref, acc_ref,
                         *, tq, tk, scale):
    qi = pl.program_id(2)
    hpg = NSA_HPG
    for hh in range(hpg):
        qa_ref[hh * tq:(hh + 1) * tq, :LANE] = q_ref[hh]
        qa_ref[hh * tq:(hh + 1) * tq, LANE:] = sel_ref[0]
    _softmax_init(m_ref, acc_ref)

    def tile(kj, masked):
        k0 = pl.multiple_of(kj * tk, tk)
        ka = jnp.concatenate([k_ref[0, pl.ds(k0, tk), :], et_ref[pl.ds(k0, tk), :]], axis=1)
        v = v_ref[0, pl.ds(k0, tk), :]
        mask = _causal_mask(qi, tq, k0, tk) if masked else None
        for hh in range(hpg):
            rows = slice(hh * tq, (hh + 1) * tq)
            s = lax.dot_general(qa_ref[rows, :], ka, _NT, preferred_element_type=F32) * scale
            if masked:
                s = jnp.where(mask, s, NEG)
            _softmax_tile(s, v, m_ref, acc_ref, rows, mask)

    n_full = (qi * tq) // tk
    _loop_tiles(n_full, lambda kj: tile(kj, False), group=4)
    tile(n_full, True)
    for hh in range(hpg):
        o_ref[:, hh * LANE:(hh + 1) * LANE] = _softmax_result(acc_ref, slice(hh * tq, (hh + 1) * tq))


def _nsa_selected(proj, sel, et, *, batch, seq, q_chunk, k_chunk, v_chunk, tq=256, tk=512):
    g_n, hpg = NSA_GROUPS, NSA_HPG
    t = batch * seq
    nq = seq // tq
    assert tq & (tq - 1) == 0 and seq % tk == 0 and tk % tq == 0 and q_chunk % hpg == 0
    return pl.pallas_call(
        functools.partial(_nsa_selected_kernel, tq=tq, tk=tk, scale=HEAD_DIM ** -0.5),
        grid=(batch, g_n, nq),
        in_specs=[pl.BlockSpec((hpg, tq, LANE), lambda b, g, i: (q_chunk // hpg + g, b * nq + i, 0)),
                  pl.BlockSpec((1, tq, LANE), lambda b, g, i: (g, b * nq + i, 0)),
                  pl.BlockSpec((1, seq, LANE), lambda b, g, i: (k_chunk + g, b, 0)),
                  pl.BlockSpec((1, seq, LANE), lambda b, g, i: (v_chunk + g, b, 0)),
                  pl.BlockSpec((seq, LANE), lambda b, g, i: (0, 0))],
        out_specs=pl.BlockSpec((tq, hpg * LANE), lambda b, g, i: (b * nq + i, g)),
        out_shape=jax.ShapeDtypeStruct((t, g_n * hpg * LANE), F32),
        scratch_shapes=[pltpu.VMEM((hpg * tq, 2 * LANE), BF16), pltpu.VMEM((hpg * tq, LANE), F32),
                        pltpu.VMEM((hpg * tq, 2 * LANE), F32)],
        compiler_params=_params("parallel", "parallel", "arbitrary"),
        name="nsa_selected",
    )(proj, sel, proj, proj, et)


def _nsa_window_kernel(q_ref, k_ref, v_ref, o_ref, m_ref, acc_ref, *, tile, scale):
    qi = pl.program_id(2)
    hpg = NSA_HPG
    _softmax_init(m_ref, acc_ref)
    span = WINDOW + tile
    k0 = pl.multiple_of(jnp.maximum(qi * tile - WINDOW, 0), tile)
    k = k_ref[0, pl.ds(k0, span), :]
    v = v_ref[0, pl.ds(k0, span), :]
    qpos = qi * tile + lax.broadcasted_iota(jnp.int32, (tile, span), 0)
    kpos = k0 + lax.broadcasted_iota(jnp.int32, (tile, span), 1)
    mask = (kpos <= qpos) & (kpos > qpos - WINDOW)
    for hh in range(hpg):
        rows = slice(hh * tile, (hh + 1) * tile)
        s = lax.dot_general(q_ref[hh], k, _NT, preferred_element_type=F32) * scale
        _softmax_tile(jnp.where(mask, s, NEG), v, m_ref, acc_ref, rows, mask)
        o_ref[:, hh * LANE:(hh + 1) * LANE] = _softmax_result(acc_ref, rows)


def _nsa_window(proj, *, batch, seq, q_chunk, k_chunk, v_chunk, tile=256):
    g_n, hpg = NSA_GROUPS, NSA_HPG
    t = batch * seq
    nq = seq // tile
    assert tile & (tile - 1) == 0 and WINDOW % tile == 0 and q_chunk % hpg == 0 and seq >= WINDOW + tile
    return pl.pallas_call(
        functools.partial(_nsa_window_kernel, tile=tile, scale=HEAD_DIM ** -0.5),
        grid=(batch, g_n, nq),
        in_specs=[pl.BlockSpec((hpg, tile, LANE), lambda b, g, i: (q_chunk // hpg + g, b * nq + i, 0)),
                  pl.BlockSpec((1, seq, LANE), lambda b, g, i: (k_chunk + g, b, 0)),
                  pl.BlockSpec((1, seq, LANE), lambda b, g, i: (v_chunk + g, b, 0))],
        out_specs=pl.BlockSpec((tile, hpg * LANE), lambda b, g, i: (b * nq + i, g)),
        out_shape=jax.ShapeDtypeStruct((t, g_n * hpg * LANE), F32),
        scratch_shapes=[pltpu.VMEM((hpg * tile, LANE), F32), pltpu.VMEM((hpg * tile, 2 * LANE), F32)],
        compiler_params=_params("parallel", "parallel", "arbitrary"),
        name="nsa_window",
    )(proj, proj, proj)


def _nsa_out_kernel(oc_ref, os_ref, ow_ref, gate_ref, w_ref, h_ref, g_ref, b_ref, out_ref):
    gates = gate_ref[...]
    parts = []
    for hh in range(N_HEADS):
        cols = slice(hh * LANE, (hh + 1) * LANE)
        parts.append(gates[:, 3 * hh:3 * hh + 1] * oc_ref[:, cols]
                     + gates[:, 3 * hh + 1:3 * hh + 2] * os_ref[:, cols]
                     + gates[:, 3 * hh + 2:3 * hh + 3] * ow_ref[:, cols])
    o = jnp.concatenate(parts, axis=1).astype(BF16)
    y = jnp.dot(o, w_ref[...], preferred_element_type=F32)
    out_ref[...] = _layer_norm(DEEPNORM_ALPHA * h_ref[...] + y, g_ref[...], b_ref[...])


def _nsa_out_ln(o_cmp, o_slc, o_win, gates, w, h, g, b, *, tm=256):
    t, d = h.shape
    row = lambda i: (i, 0)
    fixed = lambda i: (0, 0)
    return pl.pallas_call(
        _nsa_out_kernel,
        grid=(t // tm,),
        in_specs=[pl.BlockSpec((tm, d), row), pl.BlockSpec((tm, d), row), pl.BlockSpec((tm, d), row),
                  pl.BlockSpec((tm, LANE), row), pl.BlockSpec((d, d), fixed), pl.BlockSpec((tm, d), row),
                  pl.BlockSpec((1, d), fixed), pl.BlockSpec((1, d), fixed)],
        out_specs=pl.BlockSpec((tm, d), row),
        out_shape=jax.ShapeDtypeStruct((t, d), F32),
        compiler_params=_params("parallel"),
        name="nsa_out_ln",
    )(o_cmp, o_slc, o_win, gates, w, h, g.reshape(1, d), b.reshape(1, d))


def _nsa_layer(h, w_in, cmp_pe, cmp_w1, cmp_w2, w_out, ln_g, ln_b, *, batch, seq):
    nh, g_n, d = N_HEADS, NSA_GROUPS, HEAD_DIM
    dm = nh * d
    n_slc = seq // SLC_BLOCK
    n16 = seq // CMP_STRIDE
    n_cmp = (seq - CMP_BLOCK) // CMP_STRIDE + 1
    assert n_slc <= LANE and n_cmp <= n16
    wq = w_in[:, :dm]
    wkv = w_in[:, dm:dm + 6 * g_n * d]
    wg = jnp.pad(w_in[:, dm + 6 * g_n * d:], ((0, 0), (0, LANE - 3 * nh)))
    n_cmp_chunks = 2 * g_n
    w_all = jnp.concatenate([wq, wq, wkv[:, n_cmp_chunks * d:], wkv[:, :n_cmp_chunks * d], wg], axis=1).astype(BF16)
    kv0 = 2 * nh
    n_main = kv0 + 4 * g_n
    rotated = set(range(nh, 2 * nh)) | set(range(kv0, kv0 + g_n)) | set(range(kv0 + 2 * g_n, kv0 + 3 * g_n))
    cos, sin = _rope_tables(seq, d)

    def epilogue(c, val, aux):
        if c in rotated:
            val = _rope_chunk(val, aux[0], aux[1], d)
        if c == n_main + n_cmp_chunks:
            val = _sigmoid(val)
        return val

    proj, cmp_src, gates = _project_heads(h, w_all, [n_main, n_cmp_chunks, 1], [BF16, BF16, F32], seq=seq,
                                          aux=(cos, sin), epilogue=epilogue)
    kvc = _nsa_compress(cmp_src, cmp_pe, cmp_w1, cmp_w2, batch=batch, seq=seq, first_chunk=0)
    cs = jnp.arange(n16) * CMP_STRIDE
    ss = jnp.arange(LANE) * SLC_BLOCK
    ov = jnp.clip(jnp.minimum(cs[:, None] + CMP_BLOCK, ss[None, :] + SLC_BLOCK)
                  - jnp.maximum(cs[:, None], ss[None, :]), 0, None) / CMP_BLOCK
    ov = jnp.where((jnp.arange(n16)[:, None] < n_cmp) & (jnp.arange(LANE)[None, :] < n_slc), ov, 0.0)
    c2s = ov.astype(BF16)
    et = (jnp.arange(seq)[:, None] // SLC_BLOCK == jnp.arange(LANE)[None, :]).astype(BF16)
    o_cmp, sel = _nsa_cmp_select(proj, kvc, c2s, batch=batch, seq=seq)
    o_slc = _nsa_selected(proj, sel, et, batch=batch, seq=seq, q_chunk=nh,
                          k_chunk=kv0, v_chunk=kv0 + g_n)
    o_win = _nsa_window(proj, batch=batch, seq=seq, q_chunk=nh,
                        k_chunk=kv0 + 2 * g_n, v_chunk=kv0 + 3 * g_n)
    return _nsa_out_ln(o_cmp, o_slc, o_win, gates[0], w_out.astype(BF16), h, ln_g, ln_b)


def _moe_expert_kernel(te_ref, nu_ref, x_ref, wg_ref, wu_ref, wd_ref, out_ref, acc_ref):
    i = pl.program_id(0)
    k = pl.program_id(1)
    last = pl.num_programs(1) - 1
    used = i < nu_ref[0]

    @pl.when(used)
    def _():
        @pl.when(k == 0)
        def _():
            acc_ref[...] = jnp.zeros_like(acc_ref)

        x = x_ref[...]
        gate = jnp.dot(x, wg_ref[0, 0].astype(BF16), preferred_element_type=F32)
        up = jnp.dot(x, wu_ref[0, 0].astype(BF16), preferred_element_type=F32)
        act = (gate * _sigmoid(gate) * up).astype(BF16)
        acc_ref[...] += jnp.dot(act, wd_ref[0, 0].astype(BF16), preferred_element_type=F32)

        @pl.when(k == last)
        def _():
            out_ref[...] = acc_ref[...].astype(out_ref.dtype)

    @pl.when(jnp.logical_not(used) & (k == last))
    def _():
        out_ref[...] = jnp.zeros_like(out_ref)


def _moe_experts(xs, tile_expert, n_used, wg, wu, wd, *, layer, tm, tf=512):
    n_slots, d = xs.shape
    ffe = wg.shape[3]
    assert n_slots % tm == 0 and ffe % tf == 0
    grid_spec = pltpu.PrefetchScalarGridSpec(
        num_scalar_prefetch=2,
        grid=(n_slots // tm, ffe // tf),
        in_specs=[pl.BlockSpec((tm, d), lambda i, k, te, nu: (i, 0)),
                  pl.BlockSpec((1, 1, d, tf), lambda i, k, te, nu: (layer, te[i], 0, k)),
                  pl.BlockSpec((1, 1, d, tf), lambda i, k, te, nu: (layer, te[i], 0, k)),
                  pl.BlockSpec((1, 1, tf, d), lambda i, k, te, nu: (layer, te[i], k, 0))],
        out_specs=pl.BlockSpec((tm, d), lambda i, k, te, nu: (i, 0)),
        scratch_shapes=[pltpu.VMEM((tm, d), F32)])
    return pl.pallas_call(
        _moe_expert_kernel,
        grid_spec=grid_spec,
        out_shape=jax.ShapeDtypeStruct((n_slots, d), BF16),
        compiler_params=_params("parallel", "arbitrary"),
        name="moe_experts",
    )(tile_expert, n_used, xs, wg, wu, wd)


def _moe_combine_kernel(y0_ref, y1_ref, gate_ref, h_ref, g_ref, b_ref, out_ref):
    gates = gate_ref[...]
    y = gates[:, 0:1] * y0_ref[...] + gates[:, 1:2] * y1_ref[...]
    out_ref[...] = _layer_norm(DEEPNORM_ALPHA * h_ref[...] + y, g_ref[...], b_ref[...])


def _moe_combine_ln(y0, y1, gates, h, g, b, *, tm=512):
    t, d = h.shape
    row = lambda i: (i, 0)
    fixed = lambda i: (0, 0)
    return pl.pallas_call(
        _moe_combine_kernel,
        grid=(t // tm,),
        in_specs=[pl.BlockSpec((tm, d), row), pl.BlockSpec((tm, d), row), pl.BlockSpec((tm, TOP_K), row),
                  pl.BlockSpec((tm, d), row), pl.BlockSpec((1, d), fixed), pl.BlockSpec((1, d), fixed)],
        out_specs=pl.BlockSpec((tm, d), row),
        out_shape=jax.ShapeDtypeStruct((t, d), F32),
        compiler_params=_params("parallel"),
        name="moe_combine_ln",
    )(y0, y1, gates, h, g.reshape(1, d), b.reshape(1, d))


def _moe_layer(h, h_b, logits, wg, wu, wd, ln_g, ln_b, *, layer, tm=1024):
    t, d = h.shape
    ne = N_EXPERTS
    top_val, top_idx = lax.top_k(logits[:, :ne], TOP_K)
    gates = jax.nn.softmax(top_val, axis=-1)
    n_assign = t * TOP_K
    flat_e = top_idx.reshape(n_assign).astype(jnp.int32)
    flat_tok = jnp.repeat(jnp.arange(t, dtype=jnp.int32), TOP_K)
    onehot = (flat_e[:, None] == jnp.arange(ne, dtype=jnp.int32)[None, :]).astype(jnp.int32)
    before = jnp.cumsum(onehot, axis=0) - onehot
    rank = jnp.sum(before * onehot, axis=1)
    counts = jnp.sum(onehot, axis=0)
    padded = (counts + tm - 1) // tm * tm
    pad_end = jnp.cumsum(padded)
    pad_start = pad_end - padded
    grp_start = jnp.cumsum(counts) - counts
    dest = (pad_start[flat_e] + rank).astype(jnp.int32).reshape(t, TOP_K)
    n_tiles = -(-n_assign // tm) + ne
    tile_expert = jnp.minimum(jnp.searchsorted(pad_end, jnp.arange(n_tiles) * tm, side='right'),
                              ne - 1).astype(jnp.int32)
    n_used = (pad_end[-1] // tm).astype(jnp.int32).reshape(1)
    sorted_tok = flat_tok[jnp.argsort(flat_e, stable=True)]
    slot_e = jnp.repeat(tile_expert, tm)
    local = jnp.arange(n_tiles * tm, dtype=jnp.int32) - pad_start[slot_e].astype(jnp.int32)
    real = local < counts[slot_e]
    src = jnp.clip(grp_start[slot_e].astype(jnp.int32) + local, 0, n_assign - 1)
    slot_tok = jnp.where(real, sorted_tok[src], 0)
    xs = h_b[slot_tok]
    ys = _moe_experts(xs, tile_expert, n_used, wg, wu, wd, layer=layer, tm=tm)
    return _moe_combine_ln(ys[dest[:, 0]], ys[dest[:, 1]], gates, h, ln_g, ln_b)


def kernel(x, hg_w_in, hg_lb, hg_norm_g, hg_w_out, da_w_in, da_lam, da_norm_g, da_w_out, nsa_w_in, nsa_cmp_pe, nsa_cmp_w1, nsa_cmp_w2, nsa_w_out, sb_w_in, sb_w_out, ffn_w_gate, ffn_w_up, ffn_w_down, moe_w_router, moe_w_gate, moe_w_up, moe_w_down, ln_g, ln_b):
    batch, seq, d = x.shape
    h = x.reshape(batch * seq, d)

    def router(j):
        return jnp.pad(moe_w_router[j], ((0, 0), (0, LANE - N_EXPERTS))).astype(BF16)

    def dense_ffn(h, j, layer):
        return _ffn_ln(h, ffn_w_gate[j].astype(BF16), ffn_w_up[j].astype(BF16), ffn_w_down[j].astype(BF16),
                       ln_g[layer, 1], ln_b[layer, 1])

    def expert_ffn(h, h_b, logits, j, layer):
        return _moe_layer(h, h_b, logits, moe_w_gate, moe_w_up, moe_w_down, ln_g[layer, 1], ln_b[layer, 1], layer=j)

    h = _hgrn2_layer(h, hg_w_in, hg_lb, hg_norm_g, hg_w_out, ln_g[0, 0], ln_b[0, 0],
                     batch=batch, seq=seq, layer=0)
    h = dense_ffn(h, 0, 0)
    h, h_b, logits = _diff_attention_layer(h, da_w_in, da_lam, da_norm_g, da_w_out, ln_g[1, 0], ln_b[1, 0],
                                           router(0), batch=batch, seq=seq, layer=1)
    h = expert_ffn(h, h_b, logits, 0, 1)
    h = _nsa_layer(h, nsa_w_in, nsa_cmp_pe, nsa_cmp_w1, nsa_cmp_w2, nsa_w_out, ln_g[2, 0], ln_b[2, 0],
                   batch=batch, seq=seq)
    h = dense_ffn(h, 1, 2)
    h, h_b, logits = _stick_breaking_layer(h, sb_w_in, sb_w_out, ln_g[3, 0], ln_b[3, 0], router(1),
                                           batch=batch, seq=seq)
    h = expert_ffn(h, h_b, logits, 1, 3)
    return h.reshape(batch, seq, d)
```

```python
import functools
import math

import jax
import jax.numpy as jnp
from jax import lax
from jax.experimental import pallas as pl
from jax.experimental.pallas import tpu as pltpu

F32 = jnp.float32
BF16 = jnp.bfloat16

LANE = 128
VMEM_LIMIT_BYTES = 56 * 1024 * 1024

N_HEADS = 8
HEAD_DIM = 128
ROPE_THETA = 10000.0
HG_CHUNK = 64
HG_SUB = 16
DA_HALF = 64
NSA_GROUPS = 2
NSA_HPG = 4
CMP_BLOCK = 32
CMP_STRIDE = 16
SLC_BLOCK = 64
SLC_TOPK = 16
WINDOW = 512
N_EXPERTS = 8
TOP_K = 2
LN_EPS = 1e-5
NEG = -1e30
FORCE = 1e9
DEPTH = 4
DEEPNORM_ALPHA = (2 * DEPTH) ** 0.25

_NT = (((1,), (1,)), ((), ()))
_EXP_F32_ZERO_BELOW = -104.0

def _params(*semantics):
    return pltpu.CompilerParams(dimension_semantics=semantics, vmem_limit_bytes=VMEM_LIMIT_BYTES)


def _sigmoid(x):
    return 1.0 / (1.0 + jnp.exp(-x))


def _layer_norm(z, g, b):
    mu = jnp.mean(z, axis=-1, keepdims=True)
    zc = z - mu
    var = jnp.mean(zc * zc, axis=-1, keepdims=True)
    return zc * lax.rsqrt(var + LN_EPS) * g + b


def _proj_kernel(*refs, n_aux, groups, epilogue):
    x_ref, w_ref = refs[0], refs[1]
    aux_refs = refs[2:2 + n_aux]
    out_refs = refs[2 + n_aux:]
    x = x_ref[...].astype(BF16)
    aux = [r[...] for r in aux_refs]
    dest = [(out_ref, local) for out_ref, n in zip(out_refs, groups) for local in range(n)]
    width = 2
    for c0 in range(0, len(dest), width):
        n = min(width, len(dest) - c0)
        acc = jnp.dot(x, w_ref[:, c0 * LANE:(c0 + n) * LANE], preferred_element_type=F32)
        for c in range(c0, c0 + n):
            val = acc[:, (c - c0) * LANE:(c - c0 + 1) * LANE]
            if epilogue is not None:
                val = epilogue(c, val, aux)
            out_ref, local = dest[c]
            out_ref[local] = val.astype(out_ref.dtype)


def _project_heads(x, w, groups, dtypes, *, seq, aux=(), epilogue=None, tm=512):
    t, d = x.shape
    n_chunks = sum(groups)
    assert w.shape == (d, n_chunks * LANE) and t % tm == 0 and seq % tm == 0
    tiles_per_seq = seq // tm
    in_specs = [pl.BlockSpec((tm, d), lambda i: (i, 0)),
                pl.BlockSpec((d, n_chunks * LANE), lambda i: (0, 0))]
    for a in aux:
        assert a.shape == (seq, LANE)
        in_specs.append(pl.BlockSpec((tm, LANE), lambda i: (i % tiles_per_seq, 0)))
    out_shape = [jax.ShapeDtypeStruct((n, t, LANE), dt) for n, dt in zip(groups, dtypes)]
    out_specs = [pl.BlockSpec((n, tm, LANE), lambda i: (0, i, 0)) for n in groups]
    return pl.pallas_call(
        functools.partial(_proj_kernel, n_aux=len(aux), groups=tuple(groups), epilogue=epilogue),
        grid=(t // tm,),
        in_specs=in_specs,
        out_specs=out_specs,
        out_shape=out_shape,
        compiler_params=_params("parallel"),
        name="project_heads",
    )(x, w, *aux)


def _outproj_ln_kernel(*refs, with_router):
    if with_router:
        o_ref, w_ref, h_ref, g_ref, b_ref, wr_ref, out_ref, outb_ref, logit_ref = refs
    else:
        o_ref, w_ref, h_ref, g_ref, b_ref, out_ref = refs
    y = jnp.dot(o_ref[...], w_ref[...], preferred_element_type=F32)
    hn = _layer_norm(DEEPNORM_ALPHA * h_ref[...] + y, g_ref[...], b_ref[...])
    out_ref[...] = hn
    if with_router:
        hb = hn.astype(BF16)
        outb_ref[...] = hb
        logit_ref[...] = jnp.dot(hb, wr_ref[...], preferred_element_type=F32)


def _outproj_ln(o, w, h, g, b, w_router=None, *, tm=512):
    t, d = h.shape
    assert o.shape == (t, d) and t % tm == 0
    with_router = w_router is not None
    row = lambda i: (i, 0)
    fixed = lambda i: (0, 0)
    in_specs = [pl.BlockSpec((tm, d), row), pl.BlockSpec((d, d), fixed), pl.BlockSpec((tm, d), row),
                pl.BlockSpec((1, d), fixed), pl.BlockSpec((1, d), fixed)]
    args = [o, w, h, g.reshape(1, d), b.reshape(1, d)]
    out_shape = [jax.ShapeDtypeStruct((t, d), F32)]
    out_specs = [pl.BlockSpec((tm, d), row)]
    if with_router:
        in_specs.append(pl.BlockSpec((d, LANE), fixed))
        args.append(w_router)
        out_shape += [jax.ShapeDtypeStruct((t, d), BF16), jax.ShapeDtypeStruct((t, LANE), F32)]
        out_specs += [pl.BlockSpec((tm, d), row), pl.BlockSpec((tm, LANE), row)]
    res = pl.pallas_call(
        functools.partial(_outproj_ln_kernel, with_router=with_router),
        grid=(t // tm,),
        in_specs=in_specs,
        out_specs=out_specs,
        out_shape=out_shape,
        compiler_params=_params("parallel"),
        name="outproj_ln",
    )(*args)
    return res if with_router else res[0]


def _ffn_kernel(h_ref, wg_ref, wu_ref, wd_ref, g_ref, b_ref, out_ref):
    x = h_ref[...].astype(BF16)
    gate = jnp.dot(x, wg_ref[...], preferred_element_type=F32)
    up = jnp.dot(x, wu_ref[...], preferred_element_type=F32)
    act = (gate * _sigmoid(gate) * up).astype(BF16)
    y = jnp.dot(act, wd_ref[...], preferred_element_type=F32)
    out_ref[...] = _layer_norm(DEEPNORM_ALPHA * h_ref[...] + y, g_ref[...], b_ref[...])


def _ffn_ln(h, wg, wu, wd, g, b, *, tm=512):
    t, d = h.shape
    ff = wg.shape[1]
    assert t % tm == 0
    row = lambda i: (i, 0)
    fixed = lambda i: (0, 0)
    resident = pl.Buffered(1)
    return pl.pallas_call(
        _ffn_kernel,
        grid=(t // tm,),
        in_specs=[pl.BlockSpec((tm, d), row),
                  pl.BlockSpec((d, ff), fixed, pipeline_mode=resident),
                  pl.BlockSpec((d, ff), fixed, pipeline_mode=resident),
                  pl.BlockSpec((ff, d), fixed, pipeline_mode=resident),
                  pl.BlockSpec((1, d), fixed), pl.BlockSpec((1, d), fixed)],
        out_specs=pl.BlockSpec((tm, d), row),
        out_shape=jax.ShapeDtypeStruct((t, d), F32),
        compiler_params=_params("parallel"),
        name="swiglu_ln",
    )(h, wg, wu, wd, g.reshape(1, d), b.reshape(1, d))


def _hgrn_kernel(lb_ref, ng_ref, q_ref, f_ref, i_ref, g_ref, o_ref, state_ref, *, rows, hp):
    c, sub = HG_CHUNK, HG_SUB

    @pl.when(pl.program_id(2) == 0)
    def _():
        state_ref[...] = jnp.zeros_like(state_ref)

    ng = ng_ref[...]
    r_io = lax.broadcasted_iota(jnp.int32, (c, c), 0)
    c_io = lax.broadcasted_iota(jnp.int32, (c, c), 1)
    tri = (r_io >= c_io).astype(BF16)
    row_c = lax.broadcasted_iota(jnp.int32, (c, LANE), 0)
    row_s = lax.broadcasted_iota(jnp.int32, (sub, LANE), 0)

    def head_chunk(hh, r0):
        lb = lb_ref[hh]
        qr = q_ref[hh, pl.ds(r0, c), :]
        fr = f_ref[hh, pl.ds(r0, c), :]
        v = i_ref[hh, pl.ds(r0, c), :]
        gr = g_ref[hh, pl.ds(r0, c), :]
        forget = lb + (1.0 - lb) * _sigmoid(fr)
        lf = jnp.log(forget)
        k = 1.0 - forget
        q = qr * _sigmoid(qr)
        lf_hi = lf.astype(BF16)
        lf_lo = (lf - lf_hi.astype(F32)).astype(BF16)
        cum = (jnp.dot(tri, lf_hi, preferred_element_type=F32)
               + jnp.dot(tri, lf_lo, preferred_element_type=F32))
        v_b = v.astype(BF16)
        state_t = state_ref[hh]

        qe = (q * jnp.exp(cum)).astype(BF16)
        out = lax.dot_general(qe, state_t.astype(BF16), _NT, preferred_element_type=F32)

        a_rows = [jnp.zeros((sub, c), F32)]
        for i in range(1, c // sub):
            anchor = cum[i * sub - 1:i * sub, :]
            kd = jnp.where(row_c < i * sub, k * jnp.exp(jnp.minimum(anchor - cum, 0.0)), 0.0)
            qd = q[i * sub:(i + 1) * sub] * jnp.exp(cum[i * sub:(i + 1) * sub] - anchor)
            a_rows.append(lax.dot_general(qd.astype(BF16), kd.astype(BF16), _NT,
                                          preferred_element_type=F32))
        a_off = jnp.concatenate(a_rows, axis=0).astype(BF16)
        out = out + jnp.dot(a_off, v_b, preferred_element_type=F32)

        diag = []
        for i in range(c // sub):
            sl = slice(i * sub, (i + 1) * sub)
            f_i, q_i, k_i, v_i = forget[sl], q[sl], k[sl], v[sl]
            o_i = jnp.zeros((sub, LANE), F32)
            decay = jnp.zeros((sub, LANE), F32)
            for s in range(sub - 1, -1, -1):
                carried = decay * f_i[s + 1:s + 2, :] if s + 1 < sub else decay
                decay = jnp.where(row_s == s, 1.0, carried)
                w = q_i * decay * k_i[s:s + 1, :]
                o_i = o_i + jnp.sum(w, axis=-1, keepdims=True) * v_i[s:s + 1, :]
            diag.append(o_i)
        out = out + jnp.concatenate(diag, axis=0)

        last = cum[c - 1:c, :]
        kdl = (k * jnp.exp(last - cum)).astype(BF16)
        state_ref[hh] = jnp.exp(last) * state_t + jnp.dot(v.T.astype(BF16), kdl,
                                                           preferred_element_type=F32)

        ms = jnp.mean(out * out, axis=-1, keepdims=True)
        o_ref[pl.ds(r0, c), hh * LANE:(hh + 1) * LANE] = (
            out * lax.rsqrt(ms + LN_EPS) * ng * _sigmoid(gr)).astype(o_ref.dtype)

    def chunk(ci, carry):
        r0 = pl.multiple_of(ci * c, c)
        for hh in range(hp):
            head_chunk(hh, r0)
        return carry

    lax.fori_loop(0, rows // c, chunk, 0)


def _hgrn_core(proj, lb, norm_g, *, batch, seq, rows=512, hp=4):
    nh = N_HEADS
    t = batch * seq
    assert seq % rows == 0 and rows % HG_CHUNK == 0 and nh % hp == 0
    spb = seq // rows
    ng = nh // hp

    def head_spec(offset):
        return pl.BlockSpec((hp, rows, LANE), lambda b, h, s: (offset + h, b * spb + s, 0))

    return pl.pallas_call(
        functools.partial(_hgrn_kernel, rows=rows, hp=hp),
        grid=(batch, ng, spb),
        in_specs=[pl.BlockSpec((hp, 1, LANE), lambda b, h, s: (h, 0, 0)),
                  pl.BlockSpec((1, LANE), lambda b, h, s: (0, 0)),
                  head_spec(0), head_spec(ng), head_spec(2 * ng), head_spec(3 * ng)],
        out_specs=pl.BlockSpec((rows, hp * LANE), lambda b, h, s: (b * spb + s, h)),
        out_shape=jax.ShapeDtypeStruct((t, nh * LANE), BF16),
        scratch_shapes=[pltpu.VMEM((hp, LANE, LANE), F32)],
        compiler_params=_params("parallel", "parallel", "arbitrary"),
        name="hgrn2_core",
    )(lb.reshape(nh, 1, LANE), norm_g.reshape(1, LANE), proj, proj, proj, proj)


def _hgrn2_layer(h, w_in, lb_logits, norm_g, w_out, ln_g, ln_b, *, batch, seq, layer):
    lb = jnp.cumsum(jax.nn.softmax(lb_logits.astype(F32), axis=0), axis=0)[layer]
    (proj,) = _project_heads(h, w_in.astype(BF16), [4 * N_HEADS], [F32], seq=seq)
    o = _hgrn_core(proj, lb, norm_g, batch=batch, seq=seq)
    return _outproj_ln(o, w_out.astype(BF16), h, ln_g, ln_b)


def _rope_tables(seq, rot_dim):
    half = rot_dim // 2
    inv = ROPE_THETA ** (-jnp.arange(half, dtype=F32) / half)
    ang = jnp.arange(seq).astype(F32)[:, None] * inv[None, :]
    reps = LANE // rot_dim
    cos = jnp.tile(jnp.concatenate([jnp.cos(ang), jnp.cos(ang)], axis=1), (1, reps))
    sin = jnp.tile(jnp.concatenate([-jnp.sin(ang), jnp.sin(ang)], axis=1), (1, reps))
    return cos, sin


def _rope_chunk(val, cos, sin, rot_dim):
    half = rot_dim // 2
    if rot_dim == LANE:
        partner = pltpu.roll(val, half, 1)
    else:
        lane = lax.broadcasted_iota(jnp.int32, val.shape, 1)
        partner = jnp.where(lane % rot_dim < half,
                            pltpu.roll(val, LANE - half, 1), pltpu.roll(val, half, 1))
    return val * cos + partner * sin


def _softmax_init(m_ref, acc_ref):
    m_ref[...] = jnp.full_like(m_ref, NEG)
    acc_ref[...] = jnp.zeros_like(acc_ref)


def _softmax_tile(s, v, m_ref, acc_ref, rows, mask=None):
    n_lt = s.shape[1] // LANE
    part = s[:, :LANE]
    for c in range(1, n_lt):
        part = jnp.maximum(part, s[:, c * LANE:(c + 1) * LANE])
    m_prev = m_ref[rows, :]
    m_new = jnp.maximum(m_prev, jnp.max(part, axis=-1, keepdims=True))
    alpha = jnp.exp(m_prev - m_new)
    ps = []
    for c in range(n_lt):
        cols = slice(c * LANE, (c + 1) * LANE)
        pc = jnp.exp(s[:, cols] - m_new)
        if mask is not None:
            pc = jnp.where(mask[:, cols], pc, 0.0)
        ps.append(pc.astype(BF16))
    p = jnp.concatenate(ps, axis=1)
    v_aug = jnp.concatenate([v, jnp.ones_like(v)], axis=1)
    acc_ref[rows, :] = (jnp.concatenate([alpha, alpha], axis=1) * acc_ref[rows, :]
                        + jnp.dot(p, v_aug, preferred_element_type=F32))
    m_ref[rows, :] = m_new


def _softmax_result(acc_ref, rows):
    return acc_ref[rows, :LANE] / acc_ref[rows, LANE:]


def _loop_tiles(n, step, group=2):
    def body(j, carry):
        for u in range(group):
            step(group * j + u)
        return carry

    lax.fori_loop(0, n // group, body, 0)
    size = group // 2
    while size >= 1:
        start = n // (2 * size) * (2 * size)

        @pl.when(n - start >= size)
        def _(start=start, size=size):
            for u in range(size):
                step(start + u)

        size //= 2


def _loop_tiles_while(n, step, alive):
    def body(carry):
        step(carry[0])
        return carry[0] + 1, alive()

    lax.while_loop(lambda c: (c[0] < n) & c[1], body, (jnp.int32(0), alive()))


def _causal_mask(qi, tq, k0, tk):
    qpos = qi * tq + lax.broadcasted_iota(jnp.int32, (tq, tk), 0)
    kpos = k0 + lax.broadcasted_iota(jnp.int32, (tq, tk), 1)
    return kpos <= qpos


def _diff_attn_kernel(lmb_ref, ng_ref, q_ref, k_ref, v_ref, o_ref, m_ref, acc_ref, *, tq, tk, hp, out_scale):
    qi = pl.program_id(2)
    lane = lax.broadcasted_iota(jnp.int32, (tq, LANE), 1)
    chains = []
    for hh in range(hp):
        q = q_ref[hh]
        zero = jnp.zeros_like(q)
        chains.append((hh, jnp.where(lane < DA_HALF, q, zero)))
        chains.append((hh, jnp.where(lane >= DA_HALF, q, zero)))
    _softmax_init(m_ref, acc_ref)

    def tile(kj, masked):
        k0 = pl.multiple_of(kj * tk, tk)
        mask = _causal_mask(qi, tq, k0, tk) if masked else None
        for c, (hh, qc) in enumerate(chains):
            s = lax.dot_general(qc, k_ref[hh, pl.ds(k0, tk), :], _NT, preferred_element_type=F32)
            if masked:
                s = jnp.where(mask, s, NEG)
            _softmax_tile(s, v_ref[hh, pl.ds(k0, tk), :], m_ref, acc_ref, slice(c * tq, (c + 1) * tq), mask)

    n_full = (qi * tq) // tk
    _loop_tiles(n_full, lambda kj: tile(kj, False), group=4)
    tile(n_full, True)

    for hh in range(hp):
        o = (_softmax_result(acc_ref, slice(2 * hh * tq, (2 * hh + 1) * tq))
             - lmb_ref[0] * _softmax_result(acc_ref, slice((2 * hh + 1) * tq, (2 * hh + 2) * tq)))
        ms = jnp.mean(o * o, axis=-1, keepdims=True)
        o_ref[:, hh * LANE:(hh + 1) * LANE] = (o * lax.rsqrt(ms + LN_EPS) * ng_ref[...] * out_scale).astype(o_ref.dtype)


def _diff_attn_core(qkv, lmb, norm_g, *, batch, seq, lam_init, tq=256, tk=512, hp=4):
    nh = N_HEADS
    t = batch * seq
    assert seq % tk == 0 and tk % tq == 0 and nh % hp == 0
    nq = seq // tq
    ng = nh // hp
    return pl.pallas_call(
        functools.partial(_diff_attn_kernel, tq=tq, tk=tk, hp=hp, out_scale=1.0 - lam_init),
        grid=(batch, ng, nq),
        in_specs=[pl.BlockSpec(memory_space=pltpu.SMEM),
                  pl.BlockSpec((1, LANE), lambda b, h, i: (0, 0)),
                  pl.BlockSpec((hp, tq, LANE), lambda b, h, i: (h, b * nq + i, 0)),
                  pl.BlockSpec((hp, seq, LANE), lambda b, h, i: (ng + h, b, 0)),
                  pl.BlockSpec((hp, seq, LANE), lambda b, h, i: (2 * ng + h, b, 0))],
        out_specs=pl.BlockSpec((tq, hp * LANE), lambda b, h, i: (b * nq + i, h)),
        out_shape=jax.ShapeDtypeStruct((t, nh * LANE), BF16),
        scratch_shapes=[pltpu.VMEM((2 * hp * tq, LANE), F32), pltpu.VMEM((2 * hp * tq, 2 * LANE), F32)],
        compiler_params=_params("parallel", "parallel", "arbitrary"),
        name="diff_attention",
    )(lmb.reshape(1), norm_g.reshape(1, LANE), qkv, qkv, qkv)


def _diff_attention_layer(h, w_in, lam, norm_g, w_out, ln_g, ln_b, w_router, *, batch, seq, layer):
    nh = N_HEADS
    lam_init = 0.8 - 0.6 * math.exp(-0.3 * layer)
    lf = lam.astype(F32)
    lmb = jnp.exp(jnp.sum(lf[0] * lf[1])) - jnp.exp(jnp.sum(lf[2] * lf[3])) + lam_init
    cos, sin = _rope_tables(seq, DA_HALF)
    q_scale = DA_HALF ** -0.5

    def epilogue(c, val, aux):
        if c < 2 * nh:
            val = _rope_chunk(val, aux[0], aux[1], DA_HALF)
        if c < nh:
            val = val * q_scale
        return val

    (qkv,) = _project_heads(h, w_in.astype(BF16), [3 * nh], [BF16], seq=seq, aux=(cos, sin), epilogue=epilogue)
    o = _diff_attn_core(qkv, lmb, norm_g, batch=batch, seq=seq, lam_init=lam_init)
    return _outproj_ln(o, w_out.astype(BF16), h, ln_g, ln_b, w_router)


def _stick_kernel(q_ref, k_ref, v_ref, o_ref, acc_ref, run_ref, *, tq, tk, sub, hp, scale):
    qi = pl.program_id(2)
    acc_ref[...] = jnp.zeros_like(acc_ref)
    run_ref[...] = jnp.zeros_like(run_ref)
    later = (lax.broadcasted_iota(jnp.int32, (sub, sub), 0)
             > lax.broadcasted_iota(jnp.int32, (sub, sub), 1)).astype(BF16)

    def tile(kj, masked):
        k0 = pl.multiple_of(kj * tk, tk)
        if masked:
            qpos = qi * tq + lax.broadcasted_iota(jnp.int32, (tq, tk), 0)
            strict = k0 + lax.broadcasted_iota(jnp.int32, (tq, tk), 1) < qpos
        for hh in range(hp):
            rows = slice(hh * tq, (hh + 1) * tq)
            z = lax.dot_general(q_ref[hh], k_ref[hh, pl.ds(k0, tk), :], _NT, preferred_element_type=F32) * scale
            neg_abs = pltpu.bitcast(pltpu.bitcast(z, jnp.uint32) | jnp.uint32(0x80000000), F32)
            soft = jnp.log(1.0 + jnp.exp(neg_abs))
            log_beta = jnp.minimum(z, 0.0) - soft
            log_keep = log_beta - z
            if masked:
                log_keep = jnp.where(strict, log_keep, 0.0)
            run = run_ref[rows, :]
            parts = [None] * (tk // sub)
            for si in reversed(range(tk // sub)):
                cols = slice(si * sub, (si + 1) * sub)
                lk = log_keep[:, cols]
                after = (jnp.dot(lk.astype(BF16), later, preferred_element_type=F32)
                         + jnp.concatenate([run] * (sub // LANE), axis=1))
                a = jnp.exp(log_beta[:, cols] + after)
                if masked:
                    a = jnp.where(strict[:, cols], a, 0.0)
                parts[si] = a.astype(BF16)
                run = run + jnp.sum(lk, axis=-1, keepdims=True)
            acc_ref[rows, :] += jnp.dot(jnp.concatenate(parts, axis=1), v_ref[hh, pl.ds(k0, tk), :],
                                        preferred_element_type=F32)
            run_ref[rows, :] = run

    n_full = (qi * tq) // tk
    tile(n_full, True)

    _loop_tiles_while(n_full, lambda j: tile(n_full - 1 - j, False),
                      alive=lambda: jnp.max(run_ref[...]) > _EXP_F32_ZERO_BELOW)
    for hh in range(hp):
        o_ref[:, hh * LANE:(hh + 1) * LANE] = acc_ref[hh * tq:(hh + 1) * tq, :].astype(o_ref.dtype)


def _stick_core(qkv, *, batch, seq, tq=256, tk=512, sub=256, hp=4):
    nh = N_HEADS
    t = batch * seq
    assert seq % tk == 0 and tk % tq == 0 and tk % sub == 0 and sub % LANE == 0 and nh % hp == 0
    nq = seq // tq
    ng = nh // hp
    return pl.pallas_call(
        functools.partial(_stick_kernel, tq=tq, tk=tk, sub=sub, hp=hp, scale=HEAD_DIM ** -0.5),
        grid=(batch, ng, nq),
        in_specs=[pl.BlockSpec((hp, tq, LANE), lambda b, h, i: (h, b * nq + i, 0)),
                  pl.BlockSpec((hp, seq, LANE), lambda b, h, i: (ng + h, b, 0)),
                  pl.BlockSpec((hp, seq, LANE), lambda b, h, i: (2 * ng + h, b, 0))],
        out_specs=pl.BlockSpec((tq, hp * LANE), lambda b, h, i: (b * nq + i, h)),
        out_shape=jax.ShapeDtypeStruct((t, nh * LANE), BF16),
        scratch_shapes=[pltpu.VMEM((hp * tq, LANE), F32), pltpu.VMEM((hp * tq, LANE), F32)],
        compiler_params=_params("parallel", "parallel", "arbitrary"),
        name="stick_breaking",
    )(qkv, qkv, qkv)


def _stick_breaking_layer(h, w_in, w_out, ln_g, ln_b, w_router, *, batch, seq):
    (qkv,) = _project_heads(h, w_in.astype(BF16), [3 * N_HEADS], [BF16], seq=seq)
    o = _stick_core(qkv, batch=batch, seq=seq)
    return _outproj_ln(o, w_out.astype(BF16), h, ln_g, ln_b, w_router)


def _nsa_compress_kernel(x_ref, w1a_ref, w1b_ref, pe_ref, w2_ref, o_ref):
    x = x_ref[0]
    first = jnp.dot(x, w1a_ref[0, 0], preferred_element_type=F32)
    second = jnp.dot(x, w1b_ref[0, 0], preferred_element_type=F32)
    n16 = x.shape[0]
    second = pltpu.roll(second, n16 - 1, 0)
    pe = jnp.broadcast_to(pe_ref[0], (8, pe_ref.shape[2])).astype(BF16)
    bias = (jnp.dot(pe[:, :x.shape[1]], w1a_ref[0, 0], preferred_element_type=F32)
            + jnp.dot(pe[:, x.shape[1]:], w1b_ref[0, 0], preferred_element_type=F32))
    pre = first + second + bias[0:1, :]
    hid = jax.nn.gelu(pre)
    o_ref[0] = jnp.dot(hid.astype(BF16), w2_ref[0], preferred_element_type=F32).astype(o_ref.dtype)


def _nsa_compress(proj, pe, w1, w2, *, batch, seq, first_chunk):
    g_n = NSA_GROUPS
    n16 = seq // CMP_STRIDE
    half = CMP_STRIDE * LANE
    nch = proj.shape[0]
    x16 = proj.reshape(nch, batch * n16, half)
    w1r = w1.astype(BF16).reshape(2, 2, half, LANE)
    per = pe.astype(F32).reshape(2, 1, 2 * half)
    return pl.pallas_call(
        _nsa_compress_kernel,
        grid=(2, batch, g_n),
        in_specs=[pl.BlockSpec((1, n16, half), lambda j, b, g: (first_chunk + j * g_n + g, b, 0)),
                  pl.BlockSpec((1, 1, half, LANE), lambda j, b, g: (j, 0, 0, 0)),
                  pl.BlockSpec((1, 1, half, LANE), lambda j, b, g: (j, 1, 0, 0)),
                  pl.BlockSpec((1, 1, 2 * half), lambda j, b, g: (j, 0, 0)),
                  pl.BlockSpec((1, LANE, LANE), lambda j, b, g: (j, 0, 0))],
        out_specs=pl.BlockSpec((1, n16, LANE), lambda j, b, g: ((j * batch + b) * g_n + g, 0, 0)),
        out_shape=jax.ShapeDtypeStruct((2 * batch * g_n, n16, LANE), BF16),
        compiler_params=_params("parallel", "parallel", "parallel"),
        name="nsa_compress",
    )(x16, w1r, w1r, per, w2.astype(BF16))


def _nsa_cmp_select_kernel(q_ref, kc_ref, vc_ref, c2s_ref, o_ref, sel_ref, *, tq, scale):
    qi = pl.program_id(2)
    kc = kc_ref[0]
    vc = vc_ref[0]
    ncp = kc.shape[0]
    tpos = qi * tq + lax.broadcasted_iota(jnp.int32, (tq, 1), 0)
    cend = lax.broadcasted_iota(jnp.int32, (1, ncp), 1) * CMP_STRIDE + (CMP_BLOCK - 1)
    valid = cend <= tpos
    psum = jnp.zeros((tq, ncp), F32)
    for hh in range(NSA_HPG):
        s = lax.dot_general(q_ref[hh], kc, _NT, preferred_element_type=F32) * scale
        s = jnp.where(valid, s, NEG)
        m = jnp.max(s, axis=-1, keepdims=True)
        e = jnp.where(valid, jnp.exp(s - m), 0.0)
        l = jnp.sum(e, axis=-1, keepdims=True)
        p = e / jnp.where(l > 0.0, l, 1.0)
        o_ref[:, hh * LANE:(hh + 1) * LANE] = jnp.dot(p.astype(BF16), vc, preferred_element_type=F32)
        psum = psum + p
    imp = jnp.dot(psum.astype(BF16), c2s_ref[...], preferred_element_type=F32)
    blk = lax.broadcasted_iota(jnp.int32, imp.shape, 1)
    cur = tpos >> (SLC_BLOCK.bit_length() - 1)
    forced = (blk == 0) | (blk == cur) | (blk == cur - 1)
    imp = jnp.where(forced, FORCE, imp)
    imp = jnp.where(blk <= cur, imp, NEG)
    blk_f = blk.astype(F32)
    sel = jnp.zeros(imp.shape, jnp.bool_)
    x = imp
    for _ in range(SLC_TOPK):
        m = jnp.max(x, axis=-1, keepdims=True)
        idx = jnp.min(jnp.where(x == m, blk_f, float(LANE)), axis=-1, keepdims=True)
        hit = blk_f == idx
        sel = sel | (hit & (m > 0.5 * NEG))
        x = jnp.where(hit, -jnp.inf, x)
    sel_ref[0] = jnp.where(sel, 0.0, NEG).astype(sel_ref.dtype)


def _nsa_cmp_select(proj, kvc, c2s, *, batch, seq, tq=512):
    g_n, hpg = NSA_GROUPS, NSA_HPG
    t = batch * seq
    nq = seq // tq
    n16 = seq // CMP_STRIDE
    return pl.pallas_call(
        functools.partial(_nsa_cmp_select_kernel, tq=tq, scale=HEAD_DIM ** -0.5),
        grid=(batch, g_n, nq),
        in_specs=[pl.BlockSpec((hpg, tq, LANE), lambda b, g, i: (g, b * nq + i, 0)),
                  pl.BlockSpec((1, n16, LANE), lambda b, g, i: (b * g_n + g, 0, 0)),
                  pl.BlockSpec((1, n16, LANE), lambda b, g, i: ((batch + b) * g_n + g, 0, 0)),
                  pl.BlockSpec((n16, LANE), lambda b, g, i: (0, 0))],
        out_specs=[pl.BlockSpec((tq, hpg * LANE), lambda b, g, i: (b * nq + i, g)),
                   pl.BlockSpec((1, tq, LANE), lambda b, g, i: (g, b * nq + i, 0))],
        out_shape=[jax.ShapeDtypeStruct((t, g_n * hpg * LANE), F32),
                   jax.ShapeDtypeStruct((g_n, t, LANE), BF16)],
        compiler_params=_params("parallel", "parallel", "parallel"),
        name="nsa_compressed_select",
    )(proj, kvc, kvc, c2s)


def _nsa_selected_kernel(q_ref, sel_ref, k_ref, v_ref, et_ref, o_ref, qa_ref, m_ref, acc_ref,
                         *, tq, tk, scale):
    qi = pl.program_id(2)
    hpg = NSA_HPG
    for hh in range(hpg):
        qa_ref[hh * tq:(hh + 1) * tq, :LANE] = q_ref[hh]
        qa_ref[hh * tq:(hh + 1) * tq, LANE:] = sel_ref[0]
    _softmax_init(m_ref, acc_ref)

    def tile(kj, masked):
        k0 = pl.multiple_of(kj * tk, tk)
        ka = jnp.concatenate([k_ref[0, pl.ds(k0, tk), :], et_ref[pl.ds(k0, tk), :]], axis=1)
        v = v_ref[0, pl.ds(k0, tk), :]
        mask = _causal_mask(qi, tq, k0, tk) if masked else None
        for hh in range(hpg):
            rows = slice(hh * tq, (hh + 1) * tq)
            s = lax.dot_general(qa_ref[rows, :], ka, _NT, preferred_element_type=F32) * scale
            if masked:
                s = jnp.where(mask, s, NEG)
            _softmax_tile(s, v, m_ref, acc_ref, rows, mask)

    n_full = (qi * tq) // tk
    _loop_tiles(n_full, lambda kj: tile(kj, False), group=4)
    tile(n_full, True)
    for hh in range(hpg):
        o_ref[:, hh * LANE:(hh + 1) * LANE] = _softmax_result(acc_ref, slice(hh * tq, (hh + 1) * tq))


def _nsa_selected(proj, sel, et, *, batch, seq, q_chunk, k_chunk, v_chunk, tq=256, tk=512):
    g_n, hpg = NSA_GROUPS, NSA_HPG
    t = batch * seq
    nq = seq // tq
    assert tq & (tq - 1) == 0 and seq % tk == 0 and tk % tq == 0 and q_chunk % hpg == 0
    return pl.pallas_call(
        functools.partial(_nsa_selected_kernel, tq=tq, tk=tk, scale=HEAD_DIM ** -0.5),
        grid=(batch, g_n, nq),
        in_specs=[pl.BlockSpec((hpg, tq, LANE), lambda b, g, i: (q_chunk // hpg + g, b * nq + i, 0)),
                  pl.BlockSpec((1, tq, LANE), lambda b, g, i: (g, b * nq + i, 0)),
                  pl.BlockSpec((1, seq, LANE), lambda b, g, i: (k_chunk + g, b, 0)),
                  pl.BlockSpec((1, seq, LANE), lambda b, g, i: (v_chunk + g, b, 0)),
                  pl.BlockSpec((seq, LANE), lambda b, g, i: (0, 0))],
        out_specs=pl.BlockSpec((tq, hpg * LANE), lambda b, g, i: (b * nq + i, g)),
        out_shape=jax.ShapeDtypeStruct((t, g_n * hpg * LANE), F32),
        scratch_shapes=[pltpu.VMEM((hpg * tq, 2 * LANE), BF16), pltpu.VMEM((hpg * tq, LANE), F32),
                        pltpu.VMEM((hpg * tq, 2 * LANE), F32)],
        compiler_params=_params("parallel", "parallel", "arbitrary"),
        name="nsa_selected",
    )(proj, sel, proj, proj, et)


def _nsa_window_kernel(q_ref, k_ref, v_ref, o_ref, m_ref, acc_ref, *, tile, scale):
    qi = pl.program_id(2)
    hpg = NSA_HPG
    _softmax_init(m_ref, acc_ref)
    span = WINDOW + tile
    k0 = pl.multiple_of(jnp.maximum(qi * tile - WINDOW, 0), tile)
    k = k_ref[0, pl.ds(k0, span), :]
    v = v_ref[0, pl.ds(k0, span), :]
    qpos = qi * tile + lax.broadcasted_iota(jnp.int32, (tile, span), 0)
    kpos = k0 + lax.broadcasted_iota(jnp.int32, (tile, span), 1)
    mask = (kpos <= qpos) & (kpos > qpos - WINDOW)
    for hh in range(hpg):
        rows = slice(hh * tile, (hh + 1) * tile)
        s = lax.dot_general(q_ref[hh], k, _NT, preferred_element_type=F32) * scale
        _softmax_tile(jnp.where(mask, s, NEG), v, m_ref, acc_ref, rows, mask)
        o_ref[:, hh * LANE:(hh + 1) * LANE] = _softmax_result(acc_ref, rows)


def _nsa_window(proj, *, batch, seq, q_chunk, k_chunk, v_chunk, tile=256):
    g_n, hpg = NSA_GROUPS, NSA_HPG
    t = batch * seq
    nq = seq // tile
    assert tile & (tile - 1) == 0 and WINDOW % tile == 0 and q_chunk % hpg == 0 and seq >= WINDOW + tile
    return pl.pallas_call(
        functools.partial(_nsa_window_kernel, tile=tile, scale=HEAD_DIM ** -0.5),
        grid=(batch, g_n, nq),
        in_specs=[pl.BlockSpec((hpg, tile, LANE), lambda b, g, i: (q_chunk // hpg + g, b * nq + i, 0)),
                  pl.BlockSpec((1, seq, LANE), lambda b, g, i: (k_chunk + g, b, 0)),
                  pl.BlockSpec((1, seq, LANE), lambda b, g, i: (v_chunk + g, b, 0))],
        out_specs=pl.BlockSpec((tile, hpg * LANE), lambda b, g, i: (b * nq + i, g)),
        out_shape=jax.ShapeDtypeStruct((t, g_n * hpg * LANE), F32),
        scratch_shapes=[pltpu.VMEM((hpg * tile, LANE), F32), pltpu.VMEM((hpg * tile, 2 * LANE), F32)],
        compiler_params=_params("parallel", "parallel", "arbitrary"),
        name="nsa_window",
    )(proj, proj, proj)


def _nsa_out_kernel(oc_ref, os_ref, ow_ref, gate_ref, w_ref, h_ref, g_ref, b_ref, out_ref):
    gates = gate_ref[...]
    parts = []
    for hh in range(N_HEADS):
        cols = slice(hh * LANE, (hh + 1) * LANE)
        parts.append(gates[:, 3 * hh:3 * hh + 1] * oc_ref[:, cols]
                     + gates[:, 3 * hh + 1:3 * hh + 2] * os_ref[:, cols]
                     + gates[:, 3 * hh + 2:3 * hh + 3] * ow_ref[:, cols])
    o = jnp.concatenate(parts, axis=1).astype(BF16)
    y = jnp.dot(o, w_ref[...], preferred_element_type=F32)
    out_ref[...] = _layer_norm(DEEPNORM_ALPHA * h_ref[...] + y, g_ref[...], b_ref[...])


def _nsa_out_ln(o_cmp, o_slc, o_win, gates, w, h, g, b, *, tm=256):
    t, d = h.shape
    row = lambda i: (i, 0)
    fixed = lambda i: (0, 0)
    return pl.pallas_call(
        _nsa_out_kernel,
        grid=(t // tm,),
        in_specs=[pl.BlockSpec((tm, d), row), pl.BlockSpec((tm, d), row), pl.BlockSpec((tm, d), row),
                  pl.BlockSpec((tm, LANE), row), pl.BlockSpec((d, d), fixed), pl.BlockSpec((tm, d), row),
                  pl.BlockSpec((1, d), fixed), pl.BlockSpec((1, d), fixed)],
        out_specs=pl.BlockSpec((tm, d), row),
        out_shape=jax.ShapeDtypeStruct((t, d), F32),
        compiler_params=_params("parallel"),
        name="nsa_out_ln",
    )(o_cmp, o_slc, o_win, gates, w, h, g.reshape(1, d), b.reshape(1, d))


def _nsa_layer(h, w_in, cmp_pe, cmp_w1, cmp_w2, w_out, ln_g, ln_b, *, batch, seq):
    nh, g_n, d = N_HEADS, NSA_GROUPS, HEAD_DIM
    dm = nh * d
    n_slc = seq // SLC_BLOCK
    n16 = seq // CMP_STRIDE
    n_cmp = (seq - CMP_BLOCK) // CMP_STRIDE + 1
    assert n_slc <= LANE and n_cmp <= n16
    wq = w_in[:, :dm]
    wkv = w_in[:, dm:dm + 6 * g_n * d]
    wg = jnp.pad(w_in[:, dm + 6 * g_n * d:], ((0, 0), (0, LANE - 3 * nh)))
    n_cmp_chunks = 2 * g_n
    w_all = jnp.concatenate([wq, wq, wkv[:, n_cmp_chunks * d:], wkv[:, :n_cmp_chunks * d], wg], axis=1).astype(BF16)
    kv0 = 2 * nh
    n_main = kv0 + 4 * g_n
    rotated = set(range(nh, 2 * nh)) | set(range(kv0, kv0 + g_n)) | set(range(kv0 + 2 * g_n, kv0 + 3 * g_n))
    cos, sin = _rope_tables(seq, d)

    def epilogue(c, val, aux):
        if c in rotated:
            val = _rope_chunk(val, aux[0], aux[1], d)
        if c == n_main + n_cmp_chunks:
            val = _sigmoid(val)
        return val

    proj, cmp_src, gates = _project_heads(h, w_all, [n_main, n_cmp_chunks, 1], [BF16, BF16, F32], seq=seq,
                                          aux=(cos, sin), epilogue=epilogue)
    kvc = _nsa_compress(cmp_src, cmp_pe, cmp_w1, cmp_w2, batch=batch, seq=seq, first_chunk=0)
    cs = jnp.arange(n16) * CMP_STRIDE
    ss = jnp.arange(LANE) * SLC_BLOCK
    ov = jnp.clip(jnp.minimum(cs[:, None] + CMP_BLOCK, ss[None, :] + SLC_BLOCK)
                  - jnp.maximum(cs[:, None], ss[None, :]), 0, None) / CMP_BLOCK
    ov = jnp.where((jnp.arange(n16)[:, None] < n_cmp) & (jnp.arange(LANE)[None, :] < n_slc), ov, 0.0)
    c2s = ov.astype(BF16)
    et = (jnp.arange(seq)[:, None] // SLC_BLOCK == jnp.arange(LANE)[None, :]).astype(BF16)
    o_cmp, sel = _nsa_cmp_select(proj, kvc, c2s, batch=batch, seq=seq)
    o_slc = _nsa_selected(proj, sel, et, batch=batch, seq=seq, q_chunk=nh,
                          k_chunk=kv0, v_chunk=kv0 + g_n)
    o_win = _nsa_window(proj, batch=batch, seq=seq, q_chunk=nh,
                        k_chunk=kv0 + 2 * g_n, v_chunk=kv0 + 3 * g_n)
    return _nsa_out_ln(o_cmp, o_slc, o_win, gates[0], w_out.astype(BF16), h, ln_g, ln_b)


def _moe_expert_kernel(te_ref, nu_ref, x_ref, wg_ref, wu_ref, wd_ref, out_ref, acc_ref):
    i = pl.program_id(0)
    k = pl.program_id(1)
    last = pl.num_programs(1) - 1
    used = i < nu_ref[0]

    @pl.when(used)
    def _():
        @pl.when(k == 0)
        def _():
            acc_ref[...] = jnp.zeros_like(acc_ref)

        x = x_ref[...]
        gate = jnp.dot(x, wg_ref[0, 0].astype(BF16), preferred_element_type=F32)
        up = jnp.dot(x, wu_ref[0, 0].astype(BF16), preferred_element_type=F32)
        act = (gate * _sigmoid(gate) * up).astype(BF16)
        acc_ref[...] += jnp.dot(act, wd_ref[0, 0].astype(BF16), preferred_element_type=F32)

        @pl.when(k == last)
        def _():
            out_ref[...] = acc_ref[...].astype(out_ref.dtype)

    @pl.when(jnp.logical_not(used) & (k == last))
    def _():
        out_ref[...] = jnp.zeros_like(out_ref)


def _moe_experts(xs, tile_expert, n_used, wg, wu, wd, *, layer, tm, tf=896):
    n_slots, d = xs.shape
    ffe = wg.shape[3]
    assert n_slots % tm == 0 and ffe % tf == 0
    grid_spec = pltpu.PrefetchScalarGridSpec(
        num_scalar_prefetch=2,
        grid=(n_slots // tm, ffe // tf),
        in_specs=[pl.BlockSpec((tm, d), lambda i, k, te, nu: (i, 0)),
                  pl.BlockSpec((1, 1, d, tf), lambda i, k, te, nu: (layer, te[i], 0, k)),
                  pl.BlockSpec((1, 1, d, tf), lambda i, k, te, nu: (layer, te[i], 0, k)),
                  pl.BlockSpec((1, 1, tf, d), lambda i, k, te, nu: (layer, te[i], k, 0))],
        out_specs=pl.BlockSpec((tm, d), lambda i, k, te, nu: (i, 0)),
        scratch_shapes=[pltpu.VMEM((tm, d), F32)])
    return pl.pallas_call(
        _moe_expert_kernel,
        grid_spec=grid_spec,
        out_shape=jax.ShapeDtypeStruct((n_slots, d), BF16),
        compiler_params=_params("parallel", "arbitrary"),
        name="moe_experts",
    )(tile_expert, n_used, xs, wg, wu, wd)


def _moe_combine_kernel(y0_ref, y1_ref, gate_ref, h_ref, g_ref, b_ref, out_ref):
    gates = gate_ref[...]
    y = gates[:, 0:1] * y0_ref[...] + gates[:, 1:2] * y1_ref[...]
    out_ref[...] = _layer_norm(DEEPNORM_ALPHA * h_ref[...] + y, g_ref[...], b_ref[...])


def _moe_combine_ln(y0, y1, gates, h, g, b, *, tm=512):
    t, d = h.shape
    row = lambda i: (i, 0)
    fixed = lambda i: (0, 0)
    return pl.pallas_call(
        _moe_combine_kernel,
        grid=(t // tm,),
        in_specs=[pl.BlockSpec((tm, d), row), pl.BlockSpec((tm, d), row), pl.BlockSpec((tm, TOP_K), row),
                  pl.BlockSpec((tm, d), row), pl.BlockSpec((1, d), fixed), pl.BlockSpec((1, d), fixed)],
        out_specs=pl.BlockSpec((tm, d), row),
        out_shape=jax.ShapeDtypeStruct((t, d), F32),
        compiler_params=_params("parallel"),
        name="moe_combine_ln",
    )(y0, y1, gates, h, g.reshape(1, d), b.reshape(1, d))


def _moe_layer(h, h_b, logits, wg, wu, wd, ln_g, ln_b, *, layer, tm=1024):
    t, d = h.shape
    ne = N_EXPERTS
    top_val, top_idx = lax.top_k(logits[:, :ne], TOP_K)
    gates = jax.nn.softmax(top_val, axis=-1)
    n_assign = t * TOP_K
    flat_e = top_idx.reshape(n_assign).astype(jnp.int32)
    flat_tok = jnp.repeat(jnp.arange(t, dtype=jnp.int32), TOP_K)
    onehot = (flat_e[:, None] == jnp.arange(ne, dtype=jnp.int32)[None, :]).astype(jnp.int32)
    before = jnp.cumsum(onehot, axis=0) - onehot
    rank = jnp.sum(before * onehot, axis=1)
    counts = jnp.sum(onehot, axis=0)
    padded = (counts + tm - 1) // tm * tm
    pad_end = jnp.cumsum(padded)
    pad_start = pad_end - padded
    grp_start = jnp.cumsum(counts) - counts
    dest = (pad_start[flat_e] + rank).astype(jnp.int32).reshape(t, TOP_K)
    n_tiles = -(-n_assign // tm) + ne
    tile_expert = jnp.minimum(jnp.searchsorted(pad_end, jnp.arange(n_tiles) * tm, side='right'),
                              ne - 1).astype(jnp.int32)
    n_used = (pad_end[-1] // tm).astype(jnp.int32).reshape(1)
    sorted_tok = flat_tok[jnp.argsort(flat_e, stable=True)]
    slot_e = jnp.repeat(tile_expert, tm)
    local = jnp.arange(n_tiles * tm, dtype=jnp.int32) - pad_start[slot_e].astype(jnp.int32)
    real = local < counts[slot_e]
    src = jnp.clip(grp_start[slot_e].astype(jnp.int32) + local, 0, n_assign - 1)
    slot_tok = jnp.where(real, sorted_tok[src], 0)
    xs = h_b[slot_tok]
    ys = _moe_experts(xs, tile_expert, n_used, wg, wu, wd, layer=layer, tm=tm)
    return _moe_combine_ln(ys[dest[:, 0]], ys[dest[:, 1]], gates, h, ln_g, ln_b)


def kernel(x, hg_w_in, hg_lb, hg_norm_g, hg_w_out, da_w_in, da_lam, da_norm_g, da_w_out, nsa_w_in, nsa_cmp_pe, nsa_cmp_w1, nsa_cmp_w2, nsa_w_out, sb_w_in, sb_w_out, ffn_w_gate, ffn_w_up, ffn_w_down, moe_w_router, moe_w_gate, moe_w_up, moe_w_down, ln_g, ln_b):
    batch, seq, d = x.shape
    h = x.reshape(batch * seq, d)

    def router(j):
        return jnp.pad(moe_w_router[j], ((0, 0), (0, LANE - N_EXPERTS))).astype(BF16)

    def dense_ffn(h, j, layer):
        return _ffn_ln(h, ffn_w_gate[j].astype(BF16), ffn_w_up[j].astype(BF16), ffn_w_down[j].astype(BF16),
                       ln_g[layer, 1], ln_b[layer, 1])

    def expert_ffn(h, h_b, logits, j, layer):
        return _moe_layer(h, h_b, logits, moe_w_gate, moe_w_up, moe_w_down, ln_g[layer, 1], ln_b[layer, 1], layer=j)

    h = _hgrn2_layer(h, hg_w_in, hg_lb, hg_norm_g, hg_w_out, ln_g[0, 0], ln_b[0, 0],
                     batch=batch, seq=seq, layer=0)
    h = dense_ffn(h, 0, 0)
    h, h_b, logits = _diff_attention_layer(h, da_w_in, da_lam, da_norm_g, da_w_out, ln_g[1, 0], ln_b[1, 0],
                                           router(0), batch=batch, seq=seq, layer=1)
    h = expert_ffn(h, h_b, logits, 0, 1)
    h = _nsa_layer(h, nsa_w_in, nsa_cmp_pe, nsa_cmp_w1, nsa_cmp_w2, nsa_w_out, ln_g[2, 0], ln_b[2, 0],
                   batch=batch, seq=seq)
    h = dense_ffn(h, 1, 2)
    h, h_b, logits = _stick_breaking_layer(h, sb_w_in, sb_w_out, ln_g[3, 0], ln_b[3, 0], router(1),
                                           batch=batch, seq=seq)
    h = expert_ffn(h, h_b, logits, 1, 3)
    return h.reshape(batch, seq, d)
```

```python
import functools
import math

import jax
import jax.numpy as jnp
from jax import lax
from jax.experimental import pallas as pl
from jax.experimental.pallas import tpu as pltpu

F32 = jnp.float32
BF16 = jnp.bfloat16

LANE = 128
VMEM_LIMIT_BYTES = 56 * 1024 * 1024

N_HEADS = 8
HEAD_DIM = 128
ROPE_THETA = 10000.0
HG_CHUNK = 64
HG_SUB = 16
DA_HALF = 64
NSA_GROUPS = 2
NSA_HPG = 4
CMP_BLOCK = 32
CMP_STRIDE = 16
SLC_BLOCK = 64
SLC_TOPK = 16
WINDOW = 512
N_EXPERTS = 8
TOP_K = 2
LN_EPS = 1e-5
NEG = -1e30
FORCE = 1e9
DEPTH = 4
DEEPNORM_ALPHA = (2 * DEPTH) ** 0.25

_NT = (((1,), (1,)), ((), ()))
_EXP_F32_ZERO_BELOW = -104.0

def _params(*semantics):
    return pltpu.CompilerParams(dimension_semantics=semantics, vmem_limit_bytes=VMEM_LIMIT_BYTES)


def _sigmoid(x):
    return 1.0 / (1.0 + jnp.exp(-x))


def _layer_norm(z, g, b):
    mu = jnp.mean(z, axis=-1, keepdims=True)
    zc = z - mu
    var = jnp.mean(zc * zc, axis=-1, keepdims=True)
    return zc * lax.rsqrt(var + LN_EPS) * g + b


def _proj_kernel(*refs, n_aux, groups, epilogue):
    x_ref, w_ref = refs[0], refs[1]
    aux_refs = refs[2:2 + n_aux]
    out_refs = refs[2 + n_aux:]
    x = x_ref[...].astype(BF16)
    aux = [r[...] for r in aux_refs]
    dest = [(out_ref, local) for out_ref, n in zip(out_refs, groups) for local in range(n)]
    width = 2
    for c0 in range(0, len(dest), width):
        n = min(width, len(dest) - c0)
        acc = jnp.dot(x, w_ref[:, c0 * LANE:(c0 + n) * LANE], preferred_element_type=F32)
        for c in range(c0, c0 + n):
            val = acc[:, (c - c0) * LANE:(c - c0 + 1) * LANE]
            if epilogue is not None:
                val = epilogue(c, val, aux)
            out_ref, local = dest[c]
            out_ref[local] = val.astype(out_ref.dtype)


def _project_heads(x, w, groups, dtypes, *, seq, aux=(), epilogue=None, tm=512):
    t, d = x.shape
    n_chunks = sum(groups)
    assert w.shape == (d, n_chunks * LANE) and t % tm == 0 and seq % tm == 0
    tiles_per_seq = seq // tm
    in_specs = [pl.BlockSpec((tm, d), lambda i: (i, 0)),
                pl.BlockSpec((d, n_chunks * LANE), lambda i: (0, 0))]
    for a in aux:
        assert a.shape == (seq, LANE)
        in_specs.append(pl.BlockSpec((tm, LANE), lambda i: (i % tiles_per_seq, 0)))
    out_shape = [jax.ShapeDtypeStruct((n, t, LANE), dt) for n, dt in zip(groups, dtypes)]
    out_specs = [pl.BlockSpec((n, tm, LANE), lambda i: (0, i, 0)) for n in groups]
    return pl.pallas_call(
        functools.partial(_proj_kernel, n_aux=len(aux), groups=tuple(groups), epilogue=epilogue),
        grid=(t // tm,),
        in_specs=in_specs,
        out_specs=out_specs,
        out_shape=out_shape,
        compiler_params=_params("parallel"),
        name="project_heads",
    )(x, w, *aux)


def _outproj_ln_kernel(*refs, with_router):
    if with_router:
        o_ref, w_ref, h_ref, g_ref, b_ref, wr_ref, out_ref, outb_ref, logit_ref = refs
    else:
        o_ref, w_ref, h_ref, g_ref, b_ref, out_ref = refs
    y = jnp.dot(o_ref[...], w_ref[...], preferred_element_type=F32)
    hn = _layer_norm(DEEPNORM_ALPHA * h_ref[...] + y, g_ref[...], b_ref[...])
    out_ref[...] = hn
    if with_router:
        hb = hn.astype(BF16)
        outb_ref[...] = hb
        logit_ref[...] = jnp.dot(hb, wr_ref[...], preferred_element_type=F32)


def _outproj_ln(o, w, h, g, b, w_router=None, *, tm=512):
    t, d = h.shape
    assert o.shape == (t, d) and t % tm == 0
    with_router = w_router is not None
    row = lambda i: (i, 0)
    fixed = lambda i: (0, 0)
    in_specs = [pl.BlockSpec((tm, d), row), pl.BlockSpec((d, d), fixed), pl.BlockSpec((tm, d), row),
                pl.BlockSpec((1, d), fixed), pl.BlockSpec((1, d), fixed)]
    args = [o, w, h, g.reshape(1, d), b.reshape(1, d)]
    out_shape = [jax.ShapeDtypeStruct((t, d), F32)]
    out_specs = [pl.BlockSpec((tm, d), row)]
    if with_router:
        in_specs.append(pl.BlockSpec((d, LANE), fixed))
        args.append(w_router)
        out_shape += [jax.ShapeDtypeStruct((t, d), BF16), jax.ShapeDtypeStruct((t, LANE), F32)]
        out_specs += [pl.BlockSpec((tm, d), row), pl.BlockSpec((tm, LANE), row)]
    res = pl.pallas_call(
        functools.partial(_outproj_ln_kernel, with_router=with_router),
        grid=(t // tm,),
        in_specs=in_specs,
        out_specs=out_specs,
        out_shape=out_shape,
        compiler_params=_params("parallel"),
        name="outproj_ln",
    )(*args)
    return res if with_router else res[0]


def _ffn_kernel(h_ref, wg_ref, wu_ref, wd_ref, g_ref, b_ref, out_ref):
    x = h_ref[...].astype(BF16)
    gate = jnp.dot(x, wg_ref[...], preferred_element_type=F32)
    up = jnp.dot(x, wu_ref[...], preferred_element_type=F32)
    act = (gate * _sigmoid(gate) * up).astype(BF16)
    y = jnp.dot(act, wd_ref[...], preferred_element_type=F32)
    out_ref[...] = _layer_norm(DEEPNORM_ALPHA * h_ref[...] + y, g_ref[...], b_ref[...])


def _ffn_ln(h, wg, wu, wd, g, b, *, tm=512):
    t, d = h.shape
    ff = wg.shape[1]
    assert t % tm == 0
    row = lambda i: (i, 0)
    fixed = lambda i: (0, 0)
    resident = pl.Buffered(1)
    return pl.pallas_call(
        _ffn_kernel,
        grid=(t // tm,),
        in_specs=[pl.BlockSpec((tm, d), row),
                  pl.BlockSpec((d, ff), fixed, pipeline_mode=resident),
                  pl.BlockSpec((d, ff), fixed, pipeline_mode=resident),
                  pl.BlockSpec((ff, d), fixed, pipeline_mode=resident),
                  pl.BlockSpec((1, d), fixed), pl.BlockSpec((1, d), fixed)],
        out_specs=pl.BlockSpec((tm, d), row),
        out_shape=jax.ShapeDtypeStruct((t, d), F32),
        compiler_params=_params("parallel"),
        name="swiglu_ln",
    )(h, wg, wu, wd, g.reshape(1, d), b.reshape(1, d))


def _hgrn_kernel(lb_ref, ng_ref, q_ref, f_ref, i_ref, g_ref, o_ref, state_ref, *, rows, hp):
    c, sub = HG_CHUNK, HG_SUB

    @pl.when(pl.program_id(2) == 0)
    def _():
        state_ref[...] = jnp.zeros_like(state_ref)

    ng = ng_ref[...]
    r_io = lax.broadcasted_iota(jnp.int32, (c, c), 0)
    c_io = lax.broadcasted_iota(jnp.int32, (c, c), 1)
    tri = (r_io >= c_io).astype(BF16)
    row_c = lax.broadcasted_iota(jnp.int32, (c, LANE), 0)
    row_s = lax.broadcasted_iota(jnp.int32, (sub, LANE), 0)

    def head_chunk(hh, r0):
        lb = lb_ref[hh]
        qr = q_ref[hh, pl.ds(r0, c), :]
        fr = f_ref[hh, pl.ds(r0, c), :]
        v = i_ref[hh, pl.ds(r0, c), :]
        gr = g_ref[hh, pl.ds(r0, c), :]
        forget = lb + (1.0 - lb) * _sigmoid(fr)
        lf = jnp.log(forget)
        k = 1.0 - forget
        q = qr * _sigmoid(qr)
        lf_hi = lf.astype(BF16)
        lf_lo = (lf - lf_hi.astype(F32)).astype(BF16)
        cum = (jnp.dot(tri, lf_hi, preferred_element_type=F32)
               + jnp.dot(tri, lf_lo, preferred_element_type=F32))
        v_b = v.astype(BF16)
        state_t = state_ref[hh]

        qe = (q * jnp.exp(cum)).astype(BF16)
        out = lax.dot_general(qe, state_t.astype(BF16), _NT, preferred_element_type=F32)

        a_rows = [jnp.zeros((sub, c), F32)]
        for i in range(1, c // sub):
            anchor = cum[i * sub - 1:i * sub, :]
            kd = jnp.where(row_c < i * sub, k * jnp.exp(jnp.minimum(anchor - cum, 0.0)), 0.0)
            qd = q[i * sub:(i + 1) * sub] * jnp.exp(cum[i * sub:(i + 1) * sub] - anchor)
            a_rows.append(lax.dot_general(qd.astype(BF16), kd.astype(BF16), _NT,
                                          preferred_element_type=F32))
        a_off = jnp.concatenate(a_rows, axis=0).astype(BF16)
        out = out + jnp.dot(a_off, v_b, preferred_element_type=F32)

        diag = []
        for i in range(c // sub):
            sl = slice(i * sub, (i + 1) * sub)
            f_i, q_i, k_i, v_i = forget[sl], q[sl], k[sl], v[sl]
            o_i = jnp.zeros((sub, LANE), F32)
            decay = jnp.zeros((sub, LANE), F32)
            for s in range(sub - 1, -1, -1):
                carried = decay * f_i[s + 1:s + 2, :] if s + 1 < sub else decay
                decay = jnp.where(row_s == s, 1.0, carried)
                w = q_i * decay * k_i[s:s + 1, :]
                o_i = o_i + jnp.sum(w, axis=-1, keepdims=True) * v_i[s:s + 1, :]
            diag.append(o_i)
        out = out + jnp.concatenate(diag, axis=0)

        last = cum[c - 1:c, :]
        kdl = (k * jnp.exp(last - cum)).astype(BF16)
        state_ref[hh] = jnp.exp(last) * state_t + jnp.dot(v.T.astype(BF16), kdl,
                                                           preferred_element_type=F32)

        ms = jnp.mean(out * out, axis=-1, keepdims=True)
        o_ref[pl.ds(r0, c), hh * LANE:(hh + 1) * LANE] = (
            out * lax.rsqrt(ms + LN_EPS) * ng * _sigmoid(gr)).astype(o_ref.dtype)

    def chunk(ci, carry):
        r0 = pl.multiple_of(ci * c, c)
        for hh in range(hp):
            head_chunk(hh, r0)
        return carry

    lax.fori_loop(0, rows // c, chunk, 0)


def _hgrn_core(proj, lb, norm_g, *, batch, seq, rows=512, hp=4):
    nh = N_HEADS
    t = batch * seq
    assert seq % rows == 0 and rows % HG_CHUNK == 0 and nh % hp == 0
    spb = seq // rows
    ng = nh // hp

    def head_spec(offset):
        return pl.BlockSpec((hp, rows, LANE), lambda b, h, s: (offset + h, b * spb + s, 0))

    return pl.pallas_call(
        functools.partial(_hgrn_kernel, rows=rows, hp=hp),
        grid=(batch, ng, spb),
        in_specs=[pl.BlockSpec((hp, 1, LANE), lambda b, h, s: (h, 0, 0)),
                  pl.BlockSpec((1, LANE), lambda b, h, s: (0, 0)),
                  head_spec(0), head_spec(ng), head_spec(2 * ng), head_spec(3 * ng)],
        out_specs=pl.BlockSpec((rows, hp * LANE), lambda b, h, s: (b * spb + s, h)),
        out_shape=jax.ShapeDtypeStruct((t, nh * LANE), BF16),
        scratch_shapes=[pltpu.VMEM((hp, LANE, LANE), F32)],
        compiler_params=_params("parallel", "parallel", "arbitrary"),
        name="hgrn2_core",
    )(lb.reshape(nh, 1, LANE), norm_g.reshape(1, LANE), proj, proj, proj, proj)


def _hgrn2_layer(h, w_in, lb_logits, norm_g, w_out, ln_g, ln_b, *, batch, seq, layer):
    lb = jnp.cumsum(jax.nn.softmax(lb_logits.astype(F32), axis=0), axis=0)[layer]
    (proj,) = _project_heads(h, w_in.astype(BF16), [4 * N_HEADS], [F32], seq=seq)
    o = _hgrn_core(proj, lb, norm_g, batch=batch, seq=seq)
    return _outproj_ln(o, w_out.astype(BF16), h, ln_g, ln_b)


def _rope_tables(seq, rot_dim):
    half = rot_dim // 2
    inv = ROPE_THETA ** (-jnp.arange(half, dtype=F32) / half)
    ang = jnp.arange(seq).astype(F32)[:, None] * inv[None, :]
    reps = LANE // rot_dim
    cos = jnp.tile(jnp.concatenate([jnp.cos(ang), jnp.cos(ang)], axis=1), (1, reps))
    sin = jnp.tile(jnp.concatenate([-jnp.sin(ang), jnp.sin(ang)], axis=1), (1, reps))
    return cos, sin


def _rope_chunk(val, cos, sin, rot_dim):
    half = rot_dim // 2
    if rot_dim == LANE:
        partner = pltpu.roll(val, half, 1)
    else:
        lane = lax.broadcasted_iota(jnp.int32, val.shape, 1)
        partner = jnp.where(lane % rot_dim < half,
                            pltpu.roll(val, LANE - half, 1), pltpu.roll(val, half, 1))
    return val * cos + partner * sin


def _softmax_init(m_ref, acc_ref):
    m_ref[...] = jnp.full_like(m_ref, NEG)
    acc_ref[...] = jnp.zeros_like(acc_ref)


def _softmax_tile(s, v, m_ref, acc_ref, rows, mask=None):
    n_lt = s.shape[1] // LANE
    part = s[:, :LANE]
    for c in range(1, n_lt):
        part = jnp.maximum(part, s[:, c * LANE:(c + 1) * LANE])
    m_prev = m_ref[rows, :]
    m_new = jnp.maximum(m_prev, jnp.max(part, axis=-1, keepdims=True))
    alpha = jnp.exp(m_prev - m_new)
    ps = []
    for c in range(n_lt):
        cols = slice(c * LANE, (c + 1) * LANE)
        pc = jnp.exp(s[:, cols] - m_new)
        if mask is not None:
            pc = jnp.where(mask[:, cols], pc, 0.0)
        ps.append(pc.astype(BF16))
    p = jnp.concatenate(ps, axis=1)
    v_aug = jnp.concatenate([v, jnp.ones_like(v)], axis=1)
    acc_ref[rows, :] = (jnp.concatenate([alpha, alpha], axis=1) * acc_ref[rows, :]
                        + jnp.dot(p, v_aug, preferred_element_type=F32))
    m_ref[rows, :] = m_new


def _softmax_result(acc_ref, rows):
    return acc_ref[rows, :LANE] / acc_ref[rows, LANE:]


def _loop_tiles(n, step, group=2):
    def body(j, carry):
        for u in range(group):
            step(group * j + u)
        return carry

    lax.fori_loop(0, n // group, body, 0)
    size = group // 2
    while size >= 1:
        start = n // (2 * size) * (2 * size)

        @pl.when(n - start >= size)
        def _(start=start, size=size):
            for u in range(size):
                step(start + u)

        size //= 2


def _loop_tiles_while(n, step, alive):
    def body(carry):
        step(carry[0])
        return carry[0] + 1, alive()

    lax.while_loop(lambda c: (c[0] < n) & c[1], body, (jnp.int32(0), alive()))


def _causal_mask(qi, tq, k0, tk):
    qpos = qi * tq + lax.broadcasted_iota(jnp.int32, (tq, tk), 0)
    kpos = k0 + lax.broadcasted_iota(jnp.int32, (tq, tk), 1)
    return kpos <= qpos


def _diff_attn_kernel(lmb_ref, ng_ref, q_ref, k_ref, v_ref, o_ref, m_ref, acc_ref, *, tq, tk, hp, out_scale):
    qi = pl.program_id(2)
    lane = lax.broadcasted_iota(jnp.int32, (tq, LANE), 1)
    chains = []
    for hh in range(hp):
        q = q_ref[hh]
        zero = jnp.zeros_like(q)
        chains.append((hh, jnp.where(lane < DA_HALF, q, zero)))
        chains.append((hh, jnp.where(lane >= DA_HALF, q, zero)))
    _softmax_init(m_ref, acc_ref)

    def tile(kj, masked):
        k0 = pl.multiple_of(kj * tk, tk)
        mask = _causal_mask(qi, tq, k0, tk) if masked else None
        for c, (hh, qc) in enumerate(chains):
            s = lax.dot_general(qc, k_ref[hh, pl.ds(k0, tk), :], _NT, preferred_element_type=F32)
            if masked:
                s = jnp.where(mask, s, NEG)
            _softmax_tile(s, v_ref[hh, pl.ds(k0, tk), :], m_ref, acc_ref, slice(c * tq, (c + 1) * tq), mask)

    n_full = (qi * tq) // tk
    _loop_tiles(n_full, lambda kj: tile(kj, False), group=4)
    tile(n_full, True)

    for hh in range(hp):
        o = (_softmax_result(acc_ref, slice(2 * hh * tq, (2 * hh + 1) * tq))
             - lmb_ref[0] * _softmax_result(acc_ref, slice((2 * hh + 1) * tq, (2 * hh + 2) * tq)))
        ms = jnp.mean(o * o, axis=-1, keepdims=True)
        o_ref[:, hh * LANE:(hh + 1) * LANE] = (o * lax.rsqrt(ms + LN_EPS) * ng_ref[...] * out_scale).astype(o_ref.dtype)


def _diff_attn_core(qkv, lmb, norm_g, *, batch, seq, lam_init, tq=256, tk=512, hp=4):
    nh = N_HEADS
    t = batch * seq
    assert seq % tk == 0 and tk % tq == 0 and nh % hp == 0
    nq = seq // tq
    ng = nh // hp
    return pl.pallas_call(
        functools.partial(_diff_attn_kernel, tq=tq, tk=tk, hp=hp, out_scale=1.0 - lam_init),
        grid=(batch, ng, nq),
        in_specs=[pl.BlockSpec(memory_space=pltpu.SMEM),
                  pl.BlockSpec((1, LANE), lambda b, h, i: (0, 0)),
                  pl.BlockSpec((hp, tq, LANE), lambda b, h, i: (h, b * nq + i, 0)),
                  pl.BlockSpec((hp, seq, LANE), lambda b, h, i: (ng + h, b, 0)),
                  pl.BlockSpec((hp, seq, LANE), lambda b, h, i: (2 * ng + h, b, 0))],
        out_specs=pl.BlockSpec((tq, hp * LANE), lambda b, h, i: (b * nq + i, h)),
        out_shape=jax.ShapeDtypeStruct((t, nh * LANE), BF16),
        scratch_shapes=[pltpu.VMEM((2 * hp * tq, LANE), F32), pltpu.VMEM((2 * hp * tq, 2 * LANE), F32)],
        compiler_params=_params("parallel", "parallel", "arbitrary"),
        name="diff_attention",
    )(lmb.reshape(1), norm_g.reshape(1, LANE), qkv, qkv, qkv)


def _diff_attention_layer(h, w_in, lam, norm_g, w_out, ln_g, ln_b, w_router, *, batch, seq, layer):
    nh = N_HEADS
    lam_init = 0.8 - 0.6 * math.exp(-0.3 * layer)
    lf = lam.astype(F32)
    lmb = jnp.exp(jnp.sum(lf[0] * lf[1])) - jnp.exp(jnp.sum(lf[2] * lf[3])) + lam_init
    cos, sin = _rope_tables(seq, DA_HALF)
    q_scale = DA_HALF ** -0.5

    def epilogue(c, val, aux):
        if c < 2 * nh:
            val = _rope_chunk(val, aux[0], aux[1], DA_HALF)
        if c < nh:
            val = val * q_scale
        return val

    (qkv,) = _project_heads(h, w_in.astype(BF16), [3 * nh], [BF16], seq=seq, aux=(cos, sin), epilogue=epilogue)
    o = _diff_attn_core(qkv, lmb, norm_g, batch=batch, seq=seq, lam_init=lam_init)
    return _outproj_ln(o, w_out.astype(BF16), h, ln_g, ln_b, w_router)


def _stick_kernel(q_ref, k_ref, v_ref, o_ref, acc_ref, run_ref, *, tq, tk, sub, hp, scale):
    qi = pl.program_id(2)
    acc_ref[...] = jnp.zeros_like(acc_ref)
    run_ref[...] = jnp.zeros_like(run_ref)
    later = (lax.broadcasted_iota(jnp.int32, (sub, sub), 0)
             > lax.broadcasted_iota(jnp.int32, (sub, sub), 1)).astype(BF16)

    def tile(kj, masked):
        k0 = pl.multiple_of(kj * tk, tk)
        if masked:
            qpos = qi * tq + lax.broadcasted_iota(jnp.int32, (tq, tk), 0)
            strict = k0 + lax.broadcasted_iota(jnp.int32, (tq, tk), 1) < qpos
        for hh in range(hp):
            rows = slice(hh * tq, (hh + 1) * tq)
            z = lax.dot_general(q_ref[hh], k_ref[hh, pl.ds(k0, tk), :], _NT, preferred_element_type=F32) * scale
            neg_abs = pltpu.bitcast(pltpu.bitcast(z, jnp.uint32) | jnp.uint32(0x80000000), F32)
            soft = jnp.log(1.0 + jnp.exp(neg_abs))
            log_beta = jnp.minimum(z, 0.0) - soft
            log_keep = log_beta - z
            if masked:
                log_keep = jnp.where(strict, log_keep, 0.0)
            run = run_ref[rows, :]
            parts = [None] * (tk // sub)
            for si in reversed(range(tk // sub)):
                cols = slice(si * sub, (si + 1) * sub)
                lk = log_keep[:, cols]
                after = (jnp.dot(lk.astype(BF16), later, preferred_element_type=F32)
                         + jnp.concatenate([run] * (sub // LANE), axis=1))
                a = jnp.exp(log_beta[:, cols] + after)
                if masked:
                    a = jnp.where(strict[:, cols], a, 0.0)
                parts[si] = a.astype(BF16)
                run = run + jnp.sum(lk, axis=-1, keepdims=True)
            acc_ref[rows, :] += jnp.dot(jnp.concatenate(parts, axis=1), v_ref[hh, pl.ds(k0, tk), :],
                                        preferred_element_type=F32)
            run_ref[rows, :] = run

    n_full = (qi * tq) // tk
    tile(n_full, True)

    _loop_tiles_while(n_full, lambda j: tile(n_full - 1 - j, False),
                      alive=lambda: jnp.max(run_ref[...]) > _EXP_F32_ZERO_BELOW)
    for hh in range(hp):
        o_ref[:, hh * LANE:(hh + 1) * LANE] = acc_ref[hh * tq:(hh + 1) * tq, :].astype(o_ref.dtype)


def _stick_core(qkv, *, batch, seq, tq=256, tk=512, sub=256, hp=4):
    nh = N_HEADS
    t = batch * seq
    assert seq % tk == 0 and tk % tq == 0 and tk % sub == 0 and sub % LANE == 0 and nh % hp == 0
    nq = seq // tq
    ng = nh // hp
    return pl.pallas_call(
        functools.partial(_stick_kernel, tq=tq, tk=tk, sub=sub, hp=hp, scale=HEAD_DIM ** -0.5),
        grid=(batch, ng, nq),
        in_specs=[pl.BlockSpec((hp, tq, LANE), lambda b, h, i: (h, b * nq + i, 0)),
                  pl.BlockSpec((hp, seq, LANE), lambda b, h, i: (ng + h, b, 0)),
                  pl.BlockSpec((hp, seq, LANE), lambda b, h, i: (2 * ng + h, b, 0))],
        out_specs=pl.BlockSpec((tq, hp * LANE), lambda b, h, i: (b * nq + i, h)),
        out_shape=jax.ShapeDtypeStruct((t, nh * LANE), BF16),
        scratch_shapes=[pltpu.VMEM((hp * tq, LANE), F32), pltpu.VMEM((hp * tq, LANE), F32)],
        compiler_params=_params("parallel", "parallel", "arbitrary"),
        name="stick_breaking",
    )(qkv, qkv, qkv)


def _stick_breaking_layer(h, w_in, w_out, ln_g, ln_b, w_router, *, batch, seq):
    (qkv,) = _project_heads(h, w_in.astype(BF16), [3 * N_HEADS], [BF16], seq=seq)
    o = _stick_core(qkv, batch=batch, seq=seq)
    return _outproj_ln(o, w_out.astype(BF16), h, ln_g, ln_b, w_router)


def _nsa_compress_kernel(x_ref, w1a_ref, w1b_ref, pe_ref, w2_ref, o_ref):
    x = x_ref[0]
    first = jnp.dot(x, w1a_ref[0, 0], preferred_element_type=F32)
    second = jnp.dot(x, w1b_ref[0, 0], preferred_element_type=F32)
    n16 = x.shape[0]
    second = pltpu.roll(second, n16 - 1, 0)
    pe = jnp.broadcast_to(pe_ref[0], (8, pe_ref.shape[2])).astype(BF16)
    bias = (jnp.dot(pe[:, :x.shape[1]], w1a_ref[0, 0], preferred_element_type=F32)
            + jnp.dot(pe[:, x.shape[1]:], w1b_ref[0, 0], preferred_element_type=F32))
    pre = first + second + bias[0:1, :]
    hid = jax.nn.gelu(pre)
    o_ref[0] = jnp.dot(hid.astype(BF16), w2_ref[0], preferred_element_type=F32).astype(o_ref.dtype)


def _nsa_compress(proj, pe, w1, w2, *, batch, seq, first_chunk):
    g_n = NSA_GROUPS
    n16 = seq // CMP_STRIDE
    half = CMP_STRIDE * LANE
    nch = proj.shape[0]
    x16 = proj.reshape(nch, batch * n16, half)
    w1r = w1.astype(BF16).reshape(2, 2, half, LANE)
    per = pe.astype(F32).reshape(2, 1, 2 * half)
    return pl.pallas_call(
        _nsa_compress_kernel,
        grid=(2, batch, g_n),
        in_specs=[pl.BlockSpec((1, n16, half), lambda j, b, g: (first_chunk + j * g_n + g, b, 0)),
                  pl.BlockSpec((1, 1, half, LANE), lambda j, b, g: (j, 0, 0, 0)),
                  pl.BlockSpec((1, 1, half, LANE), lambda j, b, g: (j, 1, 0, 0)),
                  pl.BlockSpec((1, 1, 2 * half), lambda j, b, g: (j, 0, 0)),
                  pl.BlockSpec((1, LANE, LANE), lambda j, b, g: (j, 0, 0))],
        out_specs=pl.BlockSpec((1, n16, LANE), lambda j, b, g: ((j * batch + b) * g_n + g, 0, 0)),
        out_shape=jax.ShapeDtypeStruct((2 * batch * g_n, n16, LANE), BF16),
        compiler_params=_params("parallel", "parallel", "parallel"),
        name="nsa_compress",
    )(x16, w1r, w1r, per, w2.astype(BF16))


def _nsa_cmp_select_kernel(q_ref, kc_ref, vc_ref, c2s_ref, o_ref, sel_ref, *, tq, scale):
    qi = pl.program_id(2)
    kc = kc_ref[0]
    vc = vc_ref[0]
    ncp = kc.shape[0]
    tpos = qi * tq + lax.broadcasted_iota(jnp.int32, (tq, 1), 0)
    cend = lax.broadcasted_iota(jnp.int32, (1, ncp), 1) * CMP_STRIDE + (CMP_BLOCK - 1)
    valid = cend <= tpos
    psum = jnp.zeros((tq, ncp), F32)
    for hh in range(NSA_HPG):
        s = lax.dot_general(q_ref[hh], kc, _NT, preferred_element_type=F32) * scale
        s = jnp.where(valid, s, NEG)
        m = jnp.max(s, axis=-1, keepdims=True)
        e = jnp.where(valid, jnp.exp(s - m), 0.0)
        l = jnp.sum(e, axis=-1, keepdims=True)
        p = e / jnp.where(l > 0.0, l, 1.0)
        o_ref[:, hh * LANE:(hh + 1) * LANE] = jnp.dot(p.astype(BF16), vc, preferred_element_type=F32)
        psum = psum + p
    imp = jnp.dot(psum.astype(BF16), c2s_ref[...], preferred_element_type=F32)
    blk = lax.broadcasted_iota(jnp.int32, imp.shape, 1)
    cur = tpos >> (SLC_BLOCK.bit_length() - 1)
    forced = (blk == 0) | (blk == cur) | (blk == cur - 1)
    imp = jnp.where(forced, FORCE, imp)
    imp = jnp.where(blk <= cur, imp, NEG)
    blk_f = blk.astype(F32)
    sel = jnp.zeros(imp.shape, jnp.bool_)
    x = imp
    for _ in range(SLC_TOPK):
        m = jnp.max(x, axis=-1, keepdims=True)
        idx = jnp.min(jnp.where(x == m, blk_f, float(LANE)), axis=-1, keepdims=True)
        hit = blk_f == idx
        sel = sel | (hit & (m > 0.5 * NEG))
        x = jnp.where(hit, -jnp.inf, x)
    sel_ref[0] = jnp.where(sel, 0.0, NEG).astype(sel_ref.dtype)


def _nsa_cmp_select(proj, kvc, c2s, *, batch, seq, tq=512):
    g_n, hpg = NSA_GROUPS, NSA_HPG
    t = batch * seq
    nq = seq // tq
    n16 = seq // CMP_STRIDE
    return pl.pallas_call(
        functools.partial(_nsa_cmp_select_kernel, tq=tq, scale=HEAD_DIM ** -0.5),
        grid=(batch, g_n, nq),
        in_specs=[pl.BlockSpec((hpg, tq, LANE), lambda b, g, i: (g, b * nq + i, 0)),
                  pl.BlockSpec((1, n16, LANE), lambda b, g, i: (b * g_n + g, 0, 0)),
                  pl.BlockSpec((1, n16, LANE), lambda b, g, i: ((batch + b) * g_n + g, 0, 0)),
                  pl.BlockSpec((n16, LANE), lambda b, g, i: (0, 0))],
        out_specs=[pl.BlockSpec((tq, hpg * LANE), lambda b, g, i: (b * nq + i, g)),
                   pl.BlockSpec((1, tq, LANE), lambda b, g, i: (g, b * nq + i, 0))],
        out_shape=[jax.ShapeDtypeStruct((t, g_n * hpg * LANE), F32),
                   jax.ShapeDtypeStruct((g_n, t, LANE), BF16)],
        compiler_params=_params("parallel", "parallel", "parallel"),
        name="nsa_compressed_select",
    )(proj, kvc, kvc, c2s)


def _nsa_selected_kernel(q_ref, sel_ref, k_ref, v_ref, et_ref, o_ref, qa_ref, m_ref, acc_ref,
                         *, tq, tk, scale):
    qi = pl.program_id(2)
    hpg = NSA_HPG
    for hh in range(hpg):
        qa_ref[hh * tq:(hh + 1) * tq, :LANE] = q_ref[hh]
        qa_ref[hh * tq:(hh + 1) * tq, LANE:] = sel_ref[0]
    _softmax_init(m_ref, acc_ref)

    def tile(kj, masked):
        k0 = pl.multiple_of(kj * tk, tk)
        ka = jnp.concatenate([k_ref[0, pl.ds(k0, tk), :], et_ref[pl.ds(k0, tk), :]], axis=1)
        v = v_ref[0, pl.ds(k0, tk), :]
        mask = _causal_mask(qi, tq, k0, tk) if masked else None
        for hh in range(hpg):
            rows = slice(hh * tq, (hh + 1) * tq)
            s = lax.dot_general(qa_ref[rows, :], ka, _NT, preferred_element_type=F32) * scale
            if masked:
                s = jnp.where(mask, s, NEG)
            _softmax_tile(s, v, m_ref, acc_ref, rows, mask)

    n_full = (qi * tq) // tk
    _loop_tiles(n_full, lambda kj: tile(kj, False), group=4)
    tile(n_full, True)
    for hh in range(hpg):
        o_ref[:, hh * LANE:(hh + 1) * LANE] = _softmax_result(acc_ref, slice(hh * tq, (hh + 1) * tq))


def _nsa_selected(proj, sel, et, *, batch, seq, q_chunk, k_chunk, v_chunk, tq=256, tk=512):
    g_n, hpg = NSA_GROUPS, NSA_HPG
    t = batch * seq
    nq = seq // tq
    assert tq & (tq - 1) == 0 and seq % tk == 0 and tk % tq == 0 and q_chunk % hpg == 0
    return pl.pallas_call(
        functools.partial(_nsa_selected_kernel, tq=tq, tk=tk, scale=HEAD_DIM ** -0.5),
        grid=(batch, g_n, nq),
        in_specs=[pl.BlockSpec((hpg, tq, LANE), lambda b, g, i: (q_chunk // hpg + g, b * nq + i, 0)),
                  pl.BlockSpec((1, tq, LANE), lambda b, g, i: (g, b * nq + i, 0)),
                  pl.BlockSpec((1, seq, LANE), lambda b, g, i: (k_chunk + g, b, 0)),
                  pl.BlockSpec((1, seq, LANE), lambda b, g, i: (v_chunk + g, b, 0)),
                  pl.BlockSpec((seq, LANE), lambda b, g, i: (0, 0))],
        out_specs=pl.BlockSpec((tq, hpg * LANE), lambda b, g, i: (b * nq + i, g)),
        out_shape=jax.ShapeDtypeStruct((t, g_n * hpg * LANE), F32),
        scratch_shapes=[pltpu.VMEM((hpg * tq, 2 * LANE), BF16), pltpu.VMEM((hpg * tq, LANE), F32),
                        pltpu.VMEM((hpg * tq, 2 * LANE), F32)],
        compiler_params=_params("parallel", "parallel", "arbitrary"),
        name="nsa_selected",
    )(proj, sel, proj, proj, et)


def _nsa_window_kernel(q_ref, k_ref, v_ref, o_ref, m_ref, acc_ref, *, tile, scale):
    qi = pl.program_id(2)
    hpg = NSA_HPG
    _softmax_init(m_ref, acc_ref)
    span = WINDOW + tile
    k0 = pl.multiple_of(jnp.maximum(qi * tile - WINDOW, 0), tile)
    k = k_ref[0, pl.ds(k0, span), :]
    v = v_ref[0, pl.ds(k0, span), :]
    qpos = qi * tile + lax.broadcasted_iota(jnp.int32, (tile, span), 0)
    kpos = k0 + lax.broadcasted_iota(jnp.int32, (tile, span), 1)
    mask = (kpos <= qpos) & (kpos > qpos - WINDOW)
    for hh in range(hpg):
        rows = slice(hh * tile, (hh + 1) * tile)
        s = lax.dot_general(q_ref[hh], k, _NT, preferred_element_type=F32) * scale
        _softmax_tile(jnp.where(mask, s, NEG), v, m_ref, acc_ref, rows, mask)
        o_ref[:, hh * LANE:(hh + 1) * LANE] = _softmax_result(acc_ref, rows)


def _nsa_window(proj, *, batch, seq, q_chunk, k_chunk, v_chunk, tile=256):
    g_n, hpg = NSA_GROUPS, NSA_HPG
    t = batch * seq
    nq = seq // tile
    assert tile & (tile - 1) == 0 and WINDOW % tile == 0 and q_chunk % hpg == 0 and seq >= WINDOW + tile
    return pl.pallas_call(
        functools.partial(_nsa_window_kernel, tile=tile, scale=HEAD_DIM ** -0.5),
        grid=(batch, g_n, nq),
        in_specs=[pl.BlockSpec((hpg, tile, LANE), lambda b, g, i: (q_chunk // hpg + g, b * nq + i, 0)),
                  pl.BlockSpec((1, seq, LANE), lambda b, g, i: (k_chunk + g, b, 0)),
                  pl.BlockSpec((1, seq, LANE), lambda b, g, i: (v_chunk + g, b, 0))],
        out_specs=pl.BlockSpec((tile, hpg * LANE), lambda b, g, i: (b * nq + i, g)),
        out_shape=jax.ShapeDtypeStruct((t, g_n * hpg * LANE), F32),
        scratch_shapes=[pltpu.VMEM((hpg * tile, LANE), F32), pltpu.VMEM((hpg * tile, 2 * LANE), F32)],
        compiler_params=_params("parallel", "parallel", "arbitrary"),
        name="nsa_window",
    )(proj, proj, proj)


def _nsa_out_kernel(oc_ref, os_ref, ow_ref, gate_ref, w_ref, h_ref, g_ref, b_ref, out_ref):
    gates = gate_ref[...]
    parts = []
    for hh in range(N_HEADS):
        cols = slice(hh * LANE, (hh + 1) * LANE)
        parts.append(gates[:, 3 * hh:3 * hh + 1] * oc_ref[:, cols]
                     + gates[:, 3 * hh + 1:3 * hh + 2] * os_ref[:, cols]
                     + gates[:, 3 * hh + 2:3 * hh + 3] * ow_ref[:, cols])
    o = jnp.concatenate(parts, axis=1).astype(BF16)
    y = jnp.dot(o, w_ref[...], preferred_element_type=F32)
    out_ref[...] = _layer_norm(DEEPNORM_ALPHA * h_ref[...] + y, g_ref[...], b_ref[...])


def _nsa_out_ln(o_cmp, o_slc, o_win, gates, w, h, g, b, *, tm=256):
    t, d = h.shape
    row = lambda i: (i, 0)
    fixed = lambda i: (0, 0)
    return pl.pallas_call(
        _nsa_out_kernel,
        grid=(t // tm,),
        in_specs=[pl.BlockSpec((tm, d), row), pl.BlockSpec((tm, d), row), pl.BlockSpec((tm, d), row),
                  pl.BlockSpec((tm, LANE), row), pl.BlockSpec((d, d), fixed), pl.BlockSpec((tm, d), row),
                  pl.BlockSpec((1, d), fixed), pl.BlockSpec((1, d), fixed)],
        out_specs=pl.BlockSpec((tm, d), row),
        out_shape=jax.ShapeDtypeStruct((t, d), F32),
        compiler_params=_params("parallel"),
        name="nsa_out_ln",
    )(o_cmp, o_slc, o_win, gates, w, h, g.reshape(1, d), b.reshape(1, d))


def _nsa_layer(h, w_in, cmp_pe, cmp_w1, cmp_w2, w_out, ln_g, ln_b, *, batch, seq):
    nh, g_n, d = N_HEADS, NSA_GROUPS, HEAD_DIM
    dm = nh * d
    n_slc = seq // SLC_BLOCK
    n16 = seq // CMP_STRIDE
    n_cmp = (seq - CMP_BLOCK) // CMP_STRIDE + 1
    assert n_slc <= LANE and n_cmp <= n16
    wq = w_in[:, :dm]
    wkv = w_in[:, dm:dm + 6 * g_n * d]
    wg = jnp.pad(w_in[:, dm + 6 * g_n * d:], ((0, 0), (0, LANE - 3 * nh)))
    n_cmp_chunks = 2 * g_n
    w_all = jnp.concatenate([wq, wq, wkv[:, n_cmp_chunks * d:], wkv[:, :n_cmp_chunks * d], wg], axis=1).astype(BF16)
    kv0 = 2 * nh
    n_main = kv0 + 4 * g_n
    rotated = set(range(nh, 2 * nh)) | set(range(kv0, kv0 + g_n)) | set(range(kv0 + 2 * g_n, kv0 + 3 * g_n))
    cos, sin = _rope_tables(seq, d)

    def epilogue(c, val, aux):
        if c in rotated:
            val = _rope_chunk(val, aux[0], aux[1], d)
        if c == n_main + n_cmp_chunks:
            val = _sigmoid(val)
        return val

    proj, cmp_src, gates = _project_heads(h, w_all, [n_main, n_cmp_chunks, 1], [BF16, BF16, F32], seq=seq,
                                          aux=(cos, sin), epilogue=epilogue)
    kvc = _nsa_compress(cmp_src, cmp_pe, cmp_w1, cmp_w2, batch=batch, seq=seq, first_chunk=0)
    cs = jnp.arange(n16) * CMP_STRIDE
    ss = jnp.arange(LANE) * SLC_BLOCK
    ov = jnp.clip(jnp.minimum(cs[:, None] + CMP_BLOCK, ss[None, :] + SLC_BLOCK)
                  - jnp.maximum(cs[:, None], ss[None, :]), 0, None) / CMP_BLOCK
    ov = jnp.where((jnp.arange(n16)[:, None] < n_cmp) & (jnp.arange(LANE)[None, :] < n_slc), ov, 0.0)
    c2s = ov.astype(BF16)
    et = (jnp.arange(seq)[:, None] // SLC_BLOCK == jnp.arange(LANE)[None, :]).astype(BF16)
    o_cmp, sel = _nsa_cmp_select(proj, kvc, c2s, batch=batch, seq=seq)
    o_slc = _nsa_selected(proj, sel, et, batch=batch, seq=seq, q_chunk=nh,
                          k_chunk=kv0, v_chunk=kv0 + g_n)
    o_win = _nsa_window(proj, batch=batch, seq=seq, q_chunk=nh,
                        k_chunk=kv0 + 2 * g_n, v_chunk=kv0 + 3 * g_n)
    return _nsa_out_ln(o_cmp, o_slc, o_win, gates[0], w_out.astype(BF16), h, ln_g, ln_b)


def _moe_expert_kernel(te_ref, nu_ref, x_ref, wg_ref, wu_ref, wd_ref, out_ref, acc_ref):
    i = pl.program_id(0)
    k = pl.program_id(1)
    last = pl.num_programs(1) - 1
    used = i < nu_ref[0]

    @pl.when(used)
    def _():
        @pl.when(k == 0)
        def _():
            acc_ref[...] = jnp.zeros_like(acc_ref)

        x = x_ref[...]
        gate = jnp.dot(x, wg_ref[0, 0].astype(BF16), preferred_element_type=F32)
        up = jnp.dot(x, wu_ref[0, 0].astype(BF16), preferred_element_type=F32)
        act = (gate * _sigmoid(gate) * up).astype(BF16)
        acc_ref[...] += jnp.dot(act, wd_ref[0, 0].astype(BF16), preferred_element_type=F32)

        @pl.when(k == last)
        def _():
            out_ref[...] = acc_ref[...].astype(out_ref.dtype)

    @pl.when(jnp.logical_not(used) & (k == last))
    def _():
        out_ref[...] = jnp.zeros_like(out_ref)


def _moe_experts(xs, tile_expert, n_used, wg, wu, wd, *, layer, tm, tf=512):
    n_slots, d = xs.shape
    ffe = wg.shape[3]
    assert n_slots % tm == 0 and ffe % tf == 0
    grid_spec = pltpu.PrefetchScalarGridSpec(
        num_scalar_prefetch=2,
        grid=(n_slots // tm, ffe // tf),
        in_specs=[pl.BlockSpec((tm, d), lambda i, k, te, nu: (i, 0)),
                  pl.BlockSpec((1, 1, d, tf), lambda i, k, te, nu: (layer, te[i], 0, k)),
                  pl.BlockSpec((1, 1, d, tf), lambda i, k, te, nu: (layer, te[i], 0, k)),
                  pl.BlockSpec((1, 1, tf, d), lambda i, k, te, nu: (layer, te[i], k, 0))],
        out_specs=pl.BlockSpec((tm, d), lambda i, k, te, nu: (i, 0)),
        scratch_shapes=[pltpu.VMEM((tm, d), F32)])
    return pl.pallas_call(
        _moe_expert_kernel,
        grid_spec=grid_spec,
        out_shape=jax.ShapeDtypeStruct((n_slots, d), BF16),
        compiler_params=_params("parallel", "arbitrary"),
        name="moe_experts",
    )(tile_expert, n_used, xs, wg, wu, wd)


def _moe_combine_kernel(y0_ref, y1_ref, gate_ref, h_ref, g_ref, b_ref, out_ref):
    gates = gate_ref[...]
    y = gates[:, 0:1] * y0_ref[...] + gates[:, 1:2] * y1_ref[...]
    out_ref[...] = _layer_norm(DEEPNORM_ALPHA * h_ref[...] + y, g_ref[...], b_ref[...])


def _moe_combine_ln(y0, y1, gates, h, g, b, *, tm=512):
    t, d = h.shape
    row = lambda i: (i, 0)
    fixed = lambda i: (0, 0)
    return pl.pallas_call(
        _moe_combine_kernel,
        grid=(t // tm,),
        in_specs=[pl.BlockSpec((tm, d), row), pl.BlockSpec((tm, d), row), pl.BlockSpec((tm, TOP_K), row),
                  pl.BlockSpec((tm, d), row), pl.BlockSpec((1, d), fixed), pl.BlockSpec((1, d), fixed)],
        out_specs=pl.BlockSpec((tm, d), row),
        out_shape=jax.ShapeDtypeStruct((t, d), F32),
        compiler_params=_params("parallel"),
        name="moe_combine_ln",
    )(y0, y1, gates, h, g.reshape(1, d), b.reshape(1, d))


def _moe_layer(h, h_b, logits, wg, wu, wd, ln_g, ln_b, *, layer, tm=1024):
    t, d = h.shape
    ne = N_EXPERTS
    top_val, top_idx = lax.top_k(logits[:, :ne], TOP_K)
    gates = jax.nn.softmax(top_val, axis=-1)
    n_assign = t * TOP_K
    flat_e = top_idx.reshape(n_assign).astype(jnp.int32)
    flat_tok = jnp.repeat(jnp.arange(t, dtype=jnp.int32), TOP_K)
    onehot = (flat_e[:, None] == jnp.arange(ne, dtype=jnp.int32)[None, :]).astype(jnp.int32)
    before = jnp.cumsum(onehot, axis=0) - onehot
    rank = jnp.sum(before * onehot, axis=1)
    counts = jnp.sum(onehot, axis=0)
    padded = (counts + tm - 1) // tm * tm
    pad_end = jnp.cumsum(padded)
    pad_start = pad_end - padded
    grp_start = jnp.cumsum(counts) - counts
    dest = (pad_start[flat_e] + rank).astype(jnp.int32).reshape(t, TOP_K)
    n_tiles = -(-n_assign // tm) + ne
    tile_expert = jnp.minimum(jnp.searchsorted(pad_end, jnp.arange(n_tiles) * tm, side='right'),
                              ne - 1).astype(jnp.int32)
    n_used = (pad_end[-1] // tm).astype(jnp.int32).reshape(1)
    sorted_tok = flat_tok[jnp.argsort(flat_e, stable=True)]
    slot_e = jnp.repeat(tile_expert, tm)
    local = jnp.arange(n_tiles * tm, dtype=jnp.int32) - pad_start[slot_e].astype(jnp.int32)
    real = local < counts[slot_e]
    src = jnp.clip(grp_start[slot_e].astype(jnp.int32) + local, 0, n_assign - 1)
    slot_tok = jnp.where(real, sorted_tok[src], 0)
    xs = h_b[slot_tok]
    ys = _moe_experts(xs, tile_expert, n_used, wg, wu, wd, layer=layer, tm=tm)
    return _moe_combine_ln(ys[dest[:, 0]], ys[dest[:, 1]], gates, h, ln_g, ln_b)


def kernel(x, hg_w_in, hg_lb, hg_norm_g, hg_w_out, da_w_in, da_lam, da_norm_g, da_w_out, nsa_w_in, nsa_cmp_pe, nsa_cmp_w1, nsa_cmp_w2, nsa_w_out, sb_w_in, sb_w_out, ffn_w_gate, ffn_w_up, ffn_w_down, moe_w_router, moe_w_gate, moe_w_up, moe_w_down, ln_g, ln_b):
    batch, seq, d = x.shape
    h = x.reshape(batch * seq, d)

    def router(j):
        return jnp.pad(moe_w_router[j], ((0, 0), (0, LANE - N_EXPERTS))).astype(BF16)

    def dense_ffn(h, j, layer):
        return _ffn_ln(h, ffn_w_gate[j].astype(BF16), ffn_w_up[j].astype(BF16), ffn_w_down[j].astype(BF16),
                       ln_g[layer, 1], ln_b[layer, 1])

    def expert_ffn(h, h_b, logits, j, layer):
        return _moe_layer(h, h_b, logits, moe_w_gate, moe_w_up, moe_w_down, ln_g[layer, 1], ln_b[layer, 1], layer=j)

    h = _hgrn2_layer(h, hg_w_in, hg_lb, hg_norm_g, hg_w_out, ln_g[0, 0], ln_b[0, 0],
                     batch=batch, seq=seq, layer=0)
    h = dense_ffn(h, 0, 0)
    h, h_b, logits = _diff_attention_layer(h, da_w_in, da_lam, da_norm_g, da_w_out, ln_g[1, 0], ln_b[1, 0],
                                           router(0), batch=batch, seq=seq, layer=1)
    h = expert_ffn(h, h_b, logits, 0, 1)
    h = _nsa_layer(h, nsa_w_in, nsa_cmp_pe, nsa_cmp_w1, nsa_cmp_w2, nsa_w_out, ln_g[2, 0], ln_b[2, 0],
                   batch=batch, seq=seq)
    h = dense_ffn(h, 1, 2)
    h, h_b, logits = _stick_breaking_layer(h, sb_w_in, sb_w_out, ln_g[3, 0], ln_b[3, 0], router(1),
                                           batch=batch, seq=seq)
    h = expert_ffn(h, h_b, logits, 1, 3)
    return h.reshape(batch, seq, d)
```
